```python
import math
import jax, jax.numpy as jnp
from jax import lax
import numpy as np

D_MODEL = 1024
BATCH = 4
SEQ = 8192
DEPTH = 4

N_A_LAYERS = DEPTH // 2
N_B_LAYERS = DEPTH - N_A_LAYERS
N_DENSE = (DEPTH + 1) // 2
N_MOE = DEPTH // 2

DN_ALPHA = (2.0 * DEPTH) ** 0.25
DN_BETA = (8.0 * DEPTH) ** -0.25
LN_EPS = 1e-5
RMS_EPS = 1e-6

SSM_EXPAND = 2
SSM_D_INNER = SSM_EXPAND * D_MODEL
SSM_HEAD_DIM = 64
SSM_N_HEADS = SSM_D_INNER // SSM_HEAD_DIM
SSM_N_GROUPS = 4
SSM_D_STATE = 128
SSM_CONV = 4
SSM_CHUNK = 128
SSM_GN = SSM_N_GROUPS * SSM_D_STATE
SSM_CONV_DIM = SSM_D_INNER + 2 * SSM_GN
SSM_IN_DIM = 2 * SSM_D_INNER + 2 * SSM_GN + SSM_N_HEADS
SSM_DT_MIN = 1e-3
SSM_DT_MAX = 1e-1

MLA_N_HEADS = 8
MLA_NOPE = 128
MLA_ROPE = 64
MLA_V = 128
MLA_Q_LORA = 384
MLA_KV_LORA = 256
MLA_QK = MLA_NOPE + MLA_ROPE
MLA_SCALE = MLA_QK ** -0.5
ROPE_BASE = 10000.0
Q_BLOCK = 128
MAX_POS_OFFSET = 1024

FFN_D_FF = 2816
N_EXPERTS = 8
TOP_K = 2
EXPERT_D_FF = 2816

kernel_name = "yoco_mamba2_mla_moe_deepnorm"


def layer_norm(x, g, b):
    xf = x.astype(jnp.float32)
    mu = jnp.mean(xf, axis=-1, keepdims=True)
    var = jnp.mean(jnp.square(xf - mu), axis=-1, keepdims=True)
    return ((xf - mu) * lax.rsqrt(var + LN_EPS) * g + b).astype(x.dtype)


def rms_norm(x, g):
    xf = x.astype(jnp.float32)
    xf = xf * lax.rsqrt(jnp.mean(jnp.square(xf), axis=-1, keepdims=True) + RMS_EPS)
    return (xf * g).astype(x.dtype)


def rope_cos_sin(positions):
    inv_freq = ROPE_BASE ** (-jnp.arange(0, MLA_ROPE, 2, dtype=jnp.float32) / MLA_ROPE)
    ang = positions.astype(jnp.float32)[..., None] * inv_freq
    return jnp.cos(ang), jnp.sin(ang)


def apply_rope(x, cos, sin):
    xf = x.astype(jnp.float32)
    x1, x2 = jnp.split(xf, 2, axis=-1)
    return jnp.concatenate([x1 * cos - x2 * sin, x2 * cos + x1 * sin], axis=-1).astype(x.dtype)


def swiglu(x, w_gate, w_up, w_down):
    return (jax.nn.silu(x @ w_gate) * (x @ w_up)) @ w_down


def causal_depthwise_conv(u, w, b):
    k = w.shape[0]
    out = lax.conv_general_dilated(
        u, w[:, None, :], window_strides=(1,), padding=[(k - 1, 0)],
        dimension_numbers=("NWC", "WIO", "NWC"), feature_group_count=u.shape[-1])
    return out + b


def ssd_chunked(xs, dt, a, bm, cm):
    f32 = jnp.float32
    bsz, s, h, p = xs.shape
    g, n = bm.shape[2], bm.shape[3]
    e = h // g
    nc = s // SSM_CHUNK
    l = SSM_CHUNK
    xd = (xs.astype(f32) * dt[..., None]).reshape(bsz, nc, l, g, e, p)
    la = jnp.transpose((dt * a).reshape(bsz, nc, l, g, e), (0, 1, 3, 4, 2))
    cum = jnp.cumsum(la, axis=-1)
    bc = bm.astype(f32).reshape(bsz, nc, l, g, n)
    cc = cm.astype(f32).reshape(bsz, nc, l, g, n)
    causal = jnp.tril(jnp.ones((l, l), dtype=bool))
    seg = cum[..., :, None] - cum[..., None, :]
    decay = jnp.exp(jnp.where(causal, seg, -jnp.inf))
    cb = jnp.einsum("bclgn,bcsgn->bcgls", cc, bc)
    y_diag = jnp.einsum("bcgels,bcsgep->bclgep", cb[:, :, :, None] * decay, xd)
    decay_to_end = jnp.exp(cum[..., -1:] - cum)
    states = jnp.einsum("bclgn,bcgel,bclgep->bcgepn", bc, decay_to_end, xd)
    chunk_decay = jnp.exp(cum[..., -1])

    def step(carry, inp):
        st, dec = inp
        return carry * dec[..., None, None] + st, carry

    h0 = jnp.zeros((bsz, g, e, p, n), f32)
    _, prev = lax.scan(step, h0, (jnp.moveaxis(states, 1, 0), jnp.moveaxis(chunk_decay, 1, 0)))
    prev = jnp.moveaxis(prev, 0, 1)
    y_off = jnp.einsum("bclgn,bcgepn,bcgel->bclgep", cc, prev, jnp.exp(cum))
    return (y_diag + y_off).reshape(bsz, s, h, p)


def mamba2_mixer(x, w_in, conv_w, conv_b, dt_bias, a_log, d_skip, norm_g, w_out):
    bsz, s, _ = x.shape
    zxbcdt = x @ w_in
    z, xbc, dt = jnp.split(zxbcdt, [SSM_D_INNER, SSM_D_INNER + SSM_CONV_DIM], axis=-1)
    xbc = jax.nn.silu(causal_depthwise_conv(xbc, conv_w, conv_b))
    xs, bm, cm = jnp.split(xbc, [SSM_D_INNER, SSM_D_INNER + SSM_GN], axis=-1)
    xs = xs.reshape(bsz, s, SSM_N_HEADS, SSM_HEAD_DIM)
    bm = bm.reshape(bsz, s, SSM_N_GROUPS, SSM_D_STATE)
    cm = cm.reshape(bsz, s, SSM_N_GROUPS, SSM_D_STATE)
    dt = jax.nn.softplus(dt.astype(jnp.float32) + dt_bias)
    a = -jnp.exp(a_log.astype(jnp.float32))
    y = ssd_chunked(xs, dt, a, bm, cm) + d_skip[:, None] * xs
    y = y.reshape(bsz, s, SSM_D_INNER) * jax.nn.silu(z)
    yg = y.astype(jnp.float32).reshape(bsz, s, SSM_N_GROUPS, -1)
    yg = yg * lax.rsqrt(jnp.mean(jnp.square(yg), axis=-1, keepdims=True) + LN_EPS)
    y = (yg.reshape(bsz, s, SSM_D_INNER) * norm_g).astype(x.dtype)
    return y @ w_out


def mla_shared_kv(h, w_kv_a, kv_norm_g, w_kv_b, cos, sin):
    bsz, s, _ = h.shape
    kv_a = h @ w_kv_a
    c_kv, k_rope = jnp.split(kv_a, [MLA_KV_LORA], axis=-1)
    c_kv = rms_norm(c_kv, kv_norm_g)
    kv = (c_kv @ w_kv_b).reshape(bsz, s, MLA_N_HEADS, MLA_NOPE + MLA_V)
    k_nope, v = jnp.split(kv, [MLA_NOPE], axis=-1)
    k_rope = apply_rope(k_rope, cos, sin)
    return k_nope, k_rope, v


def mla_attention(x, k_nope, k_rope, v, w_q_a, q_norm_g, w_q_b, w_o, cos, sin):
    bsz, s, _ = x.shape
    q = rms_norm(x @ w_q_a, q_norm_g) @ w_q_b
    q = q.reshape(bsz, s, MLA_N_HEADS, MLA_QK)
    q_nope, q_rope = jnp.split(q, [MLA_NOPE], axis=-1)
    q_rope = apply_rope(q_rope, cos[:, :, None, :], sin[:, :, None, :])
    nb = s // Q_BLOCK
    k_pos = jnp.arange(s)

    def block(args):
        qn, qr, i = args
        sc = (jnp.einsum("bqhd,bkhd->bhqk", qn, k_nope)
              + jnp.einsum("bqhr,bkr->bhqk", qr, k_rope)).astype(jnp.float32) * MLA_SCALE
        q_pos = i * Q_BLOCK + jnp.arange(Q_BLOCK)
        sc = jnp.where(k_pos[None, :] <= q_pos[:, None], sc, -jnp.inf)
        pr = jax.nn.softmax(sc, axis=-1).astype(v.dtype)
        return jnp.einsum("bhqk,bkhd->bqhd", pr, v)

    qn_b = jnp.moveaxis(q_nope.reshape(bsz, nb, Q_BLOCK, MLA_N_HEADS, MLA_NOPE), 1, 0)
    qr_b = jnp.moveaxis(q_rope.reshape(bsz, nb, Q_BLOCK, MLA_N_HEADS, MLA_ROPE), 1, 0)
    o = lax.map(block, (qn_b, qr_b, jnp.arange(nb)))
    o = jnp.moveaxis(o, 0, 1).reshape(bsz, s, MLA_N_HEADS * MLA_V)
    return o @ w_o


def moe_swiglu(x, w_router, b_router, w_gate, w_up, w_down):
    logits = (x @ w_router).astype(jnp.float32) + b_router
    top_val, top_idx = lax.top_k(logits, TOP_K)
    top_w = jax.nn.softmax(top_val, axis=-1)
    gates = jnp.einsum("bsk,bske->bse", top_w,
                       jax.nn.one_hot(top_idx, N_EXPERTS, dtype=jnp.float32))
    out = jnp.zeros_like(x)
    for e in range(N_EXPERTS):
        out = out + gates[..., e:e + 1].astype(x.dtype) * swiglu(x, w_gate[e], w_up[e], w_down[e])
    return out


def setup_inputs(seed: int = 0) -> dict:
    key = jax.random.key(seed)
    ks = iter(jax.random.split(key, 40))
    f32 = jnp.float32

    def nrm(shape, scale):
        return jax.random.normal(next(ks), shape, f32) * scale

    x = jax.random.normal(next(ks), (BATCH, SEQ, D_MODEL), f32)
    offset = jax.random.randint(next(ks), (BATCH, 1), 0, MAX_POS_OFFSET, dtype=jnp.int32)
    positions = offset + jnp.arange(SEQ, dtype=jnp.int32)[None, :]
    ln_g = 1.0 + nrm((DEPTH, 2, D_MODEL), 0.02)
    ln_b = nrm((DEPTH, 2, D_MODEL), 0.02)
    ssm_w_in = nrm((N_A_LAYERS, D_MODEL, SSM_IN_DIM), D_MODEL ** -0.5)
    ssm_conv_w = nrm((N_A_LAYERS, SSM_CONV, SSM_CONV_DIM), SSM_CONV ** -0.5)
    ssm_conv_b = nrm((N_A_LAYERS, SSM_CONV_DIM), 0.02)
    dt0 = jnp.exp(jax.random.uniform(next(ks), (N_A_LAYERS, SSM_N_HEADS), f32,
                                     math.log(SSM_DT_MIN), math.log(SSM_DT_MAX)))
    ssm_dt_bias = dt0 + jnp.log(-jnp.expm1(-dt0))
    ssm_a_log = jnp.log(jax.random.uniform(next(ks), (N_A_LAYERS, SSM_N_HEADS), f32, 1.0, 16.0))
    ssm_d = 1.0 + nrm((N_A_LAYERS, SSM_N_HEADS), 0.1)
    ssm_norm_g = 1.0 + nrm((N_A_LAYERS, SSM_D_INNER), 0.02)
    ssm_w_out = nrm((N_A_LAYERS, SSM_D_INNER, D_MODEL), SSM_D_INNER ** -0.5 * DN_BETA)
    mla_w_kv_a = nrm((D_MODEL, MLA_KV_LORA + MLA_ROPE), D_MODEL ** -0.5)
    mla_kv_norm_g = 1.0 + nrm((MLA_KV_LORA,), 0.02)
    mla_w_kv_b = nrm((MLA_KV_LORA, MLA_N_HEADS * (MLA_NOPE + MLA_V)), MLA_KV_LORA ** -0.5)
    mla_w_q_a = nrm((N_B_LAYERS, D_MODEL, MLA_Q_LORA), D_MODEL ** -0.5)
    mla_q_norm_g = 1.0 + nrm((N_B_LAYERS, MLA_Q_LORA), 0.02)
    mla_w_q_b = nrm((N_B_LAYERS, MLA_Q_LORA, MLA_N_HEADS * MLA_QK), MLA_Q_LORA ** -0.5)
    mla_w_o = nrm((N_B_LAYERS, MLA_N_HEADS * MLA_V, D_MODEL), (MLA_N_HEADS * MLA_V) ** -0.5 * DN_BETA)
    ffn_w_gate = nrm((N_DENSE, D_MODEL, FFN_D_FF), D_MODEL ** -0.5)
    ffn_w_up = nrm((N_DENSE, D_MODEL, FFN_D_FF), D_MODEL ** -0.5)
    ffn_w_down = nrm((N_DENSE, FFN_D_FF, D_MODEL), FFN_D_FF ** -0.5 * DN_BETA)
    moe_w_router = nrm((N_MOE, D_MODEL, N_EXPERTS), D_MODEL ** -0.5)
    moe_b_router = nrm((N_MOE, N_EXPERTS), 0.01)
    moe_w_gate = nrm((N_MOE, N_EXPERTS, D_MODEL, EXPERT_D_FF), D_MODEL ** -0.5)
    moe_w_up = nrm((N_MOE, N_EXPERTS, D_MODEL, EXPERT_D_FF), D_MODEL ** -0.5)
    moe_w_down = nrm((N_MOE, N_EXPERTS, EXPERT_D_FF, D_MODEL), EXPERT_D_FF ** -0.5 * DN_BETA)
    return {
        "x": x, "positions": positions, "ln_g": ln_g, "ln_b": ln_b,
        "ssm_w_in": ssm_w_in, "ssm_conv_w": ssm_conv_w, "ssm_conv_b": ssm_conv_b,
        "ssm_dt_bias": ssm_dt_bias, "ssm_a_log": ssm_a_log, "ssm_d": ssm_d,
        "ssm_norm_g": ssm_norm_g, "ssm_w_out": ssm_w_out,
        "mla_w_kv_a": mla_w_kv_a, "mla_kv_norm_g": mla_kv_norm_g, "mla_w_kv_b": mla_w_kv_b,
        "mla_w_q_a": mla_w_q_a, "mla_q_norm_g": mla_q_norm_g, "mla_w_q_b": mla_w_q_b,
        "mla_w_o": mla_w_o,
        "ffn_w_gate": ffn_w_gate, "ffn_w_up": ffn_w_up, "ffn_w_down": ffn_w_down,
        "moe_w_router": moe_w_router, "moe_b_router": moe_b_router,
        "moe_w_gate": moe_w_gate, "moe_w_up": moe_w_up, "moe_w_down": moe_w_down,
    }


def reference(x, positions, ln_g, ln_b,
              ssm_w_in, ssm_conv_w, ssm_conv_b, ssm_dt_bias, ssm_a_log, ssm_d,
              ssm_norm_g, ssm_w_out,
              mla_w_kv_a, mla_kv_norm_g, mla_w_kv_b,
              mla_w_q_a, mla_q_norm_g, mla_w_q_b, mla_w_o,
              ffn_w_gate, ffn_w_up, ffn_w_down,
              moe_w_router, moe_b_router, moe_w_gate, moe_w_up, moe_w_down):
    cos, sin = rope_cos_sin(positions)
    h = x
    k_nope = k_rope = v = None
    for layer in range(DEPTH):
        if layer < N_A_LAYERS:
            a = layer
            mix = mamba2_mixer(h, ssm_w_in[a], ssm_conv_w[a], ssm_conv_b[a], ssm_dt_bias[a],
                               ssm_a_log[a], ssm_d[a], ssm_norm_g[a], ssm_w_out[a])
        else:
            bi = layer - N_A_LAYERS
            if bi == 0:
                k_nope, k_rope, v = mla_shared_kv(h, mla_w_kv_a, mla_kv_norm_g, mla_w_kv_b, cos, sin)
            mix = mla_attention(h, k_nope, k_rope, v, mla_w_q_a[bi], mla_q_norm_g[bi],
                                mla_w_q_b[bi], mla_w_o[bi], cos, sin)
        h = layer_norm(DN_ALPHA * h + mix, ln_g[layer, 0], ln_b[layer, 0])
        c = layer // 2
        if layer % 2 == 0:
            ffn = swiglu(h, ffn_w_gate[c], ffn_w_up[c], ffn_w_down[c])
        else:
            ffn = moe_swiglu(h, moe_w_router[c], moe_b_router[c], moe_w_gate[c],
                             moe_w_up[c], moe_w_down[c])
        h = layer_norm(DN_ALPHA * h + ffn, ln_g[layer, 1], ln_b[layer, 1])
    return h
```

```python
import functools
import math

import numpy as np
import jax
import jax.numpy as jnp
from jax import lax
from jax.experimental import pallas as pl
from jax.experimental.pallas import tpu as pltpu

F32 = jnp.float32
BF16 = jnp.bfloat16

DEPTH = 4
DN_ALPHA = (2.0 * DEPTH) ** 0.25
LN_EPS = 1e-5
RMS_EPS = 1e-6
SSM_HEAD_DIM = 64
SSM_N_GROUPS = 4
SSM_D_STATE = 128
SSM_CHUNK = 128
SSM_CONV = 4
MLA_N_HEADS = 8
MLA_NOPE = 128
MLA_ROPE = 64
MLA_V = 128
MLA_QK = MLA_NOPE + MLA_ROPE
MLA_SCALE = MLA_QK ** -0.5
ROPE_BASE = 10000.0
N_EXPERTS = 8
TOP_K = 2

LANES = 128
SUBLANES = 8
VMEM_LIMIT = 56 * 1024 * 1024

TOKEN_TILE = 512
FF_TILE = 1408
ATTN_TILE = 1024


def _params(*sem):
    return pltpu.CompilerParams(dimension_semantics=sem, vmem_limit_bytes=VMEM_LIMIT)


def _dot(a, b):
    return jnp.dot(a, b, preferred_element_type=F32)


def _silu(v):
    return v * (1.0 / (1.0 + jnp.exp(-v)))


def _layer_norm(v, g, b):
    mu = jnp.mean(v, axis=-1, keepdims=True)
    vc = v - mu
    var = jnp.mean(vc * vc, axis=-1, keepdims=True)
    return vc * lax.rsqrt(var + LN_EPS) * g + b


def _lane_iota(shape):
    return lax.broadcasted_iota(jnp.int32, shape, len(shape) - 1)


def _inproj_kernel(x_ref, wz_ref, wx_ref, wdt_ref, z_ref, xbc_ref, dt_ref):
    x = x_ref[...]
    z_ref[...] = _dot(x, wz_ref[...]).astype(z_ref.dtype)
    xbc_ref[...] = _dot(x, wx_ref[...]).astype(xbc_ref.dtype)
    dt_ref[...] = _dot(x, wdt_ref[...])


def _inproj(xb, wz, wx, wdt):
    t, d = xb.shape
    tm = TOKEN_TILE
    full = lambda w: pl.BlockSpec(w.shape, lambda i: (0, 0))
    row = lambda n: pl.BlockSpec((tm, n), lambda i: (i, 0))
    return pl.pallas_call(
        _inproj_kernel,
        grid=(t // tm,),
        in_specs=[row(d), full(wz), full(wx), full(wdt)],
        out_specs=[row(wz.shape[1]), row(wx.shape[1]), row(wdt.shape[1])],
        out_shape=[jax.ShapeDtypeStruct((t, wz.shape[1]), BF16),
                   jax.ShapeDtypeStruct((t, wx.shape[1]), BF16),
                   jax.ShapeDtypeStruct((t, wdt.shape[1]), F32)],
        compiler_params=_params("arbitrary"),
        name="ssm_in_proj",
    )(xb, wz, wx, wdt)


def _ssd_kernel(z_ref, xbc_ref, dt_ref, cw_ref, cb_ref, dtb_ref, alog_ref, dsk_ref, ng_ref,
                y_ref, ubuf, xc, ybuf, state, *, n_heads, d_inner, gn):
    L = SSM_CHUNK
    P = SSM_HEAD_DIM
    n_state = SSM_D_STATE
    conv_dim = d_inner + 2 * gn
    halo = SUBLANES
    c = pl.program_id(1)

    @pl.when(c == 0)
    def _():
        ubuf[0:halo, :] = jnp.zeros((halo, conv_dim), F32)
        state[...] = jnp.zeros(state.shape, F32)

    ubuf[halo:halo + L, :] = xbc_ref[...].astype(F32)
    cblk = 512
    for j in range(conv_dim // cblk):
        cs = slice(j * cblk, (j + 1) * cblk)
        acc = cb_ref[:, cs] + cw_ref[3:4, cs] * ubuf[halo:halo + L, cs]
        for k in range(SSM_CONV - 1):
            sh = SSM_CONV - 1 - k
            acc = acc + cw_ref[k:k + 1, cs] * ubuf[halo - sh:halo - sh + L, cs]
        xc[:, cs] = _silu(acc)
    ubuf[0:halo, :] = ubuf[L:L + halo, :]

    lane = _lane_iota((L, LANES))
    head_ok = lane < n_heads
    dtr = dt_ref[...] + dtb_ref[...]
    dtp = jnp.maximum(dtr, 0.0) + jnp.log1p(jnp.exp(-jnp.abs(dtr)))
    dtp = jnp.where(head_ok, dtp, 0.0)
    a_row = jnp.where(head_ok[0:1, :], -jnp.exp(alog_ref[...]), 0.0)
    la = dtp * a_row
    row_i = lax.broadcasted_iota(jnp.int32, (L, L), 0)
    col_i = lax.broadcasted_iota(jnp.int32, (L, L), 1)
    causal = row_i >= col_i
    tril = jnp.where(causal, 1.0, 0.0).astype(F32)
    cum = jnp.dot(tril, la, preferred_element_type=F32, precision=lax.Precision.HIGHEST)
    cum_t = cum.T
    dt_t = dtp.T
    ecum = jnp.exp(cum)
    last_t = cum_t[:, L - 1:L]
    w_t = jnp.exp(last_t - cum_t) * dt_t
    edec_t = jnp.exp(last_t)
    lane_lo = _lane_iota((L, 2 * P)) < P

    b_off = d_inner
    c_off = d_inner + gn
    heads_per_group = n_heads // SSM_N_GROUPS
    for g in range(SSM_N_GROUPS):
        bg = xc[:, b_off + g * n_state: b_off + (g + 1) * n_state]
        cg = xc[:, c_off + g * n_state: c_off + (g + 1) * n_state]
        bg_t = bg.T
        cb = _dot(cg.astype(BF16), bg_t.astype(BF16))
        for pr in range(heads_per_group // 2):
            h0 = g * heads_per_group + 2 * pr
            pair = h0 // 2
            xs_pair = xc[:, h0 * P:(h0 + 2) * P].astype(BF16)
            st_pair = state[pair]
            rhs = jnp.concatenate([xs_pair, st_pair.astype(BF16)], axis=0)
            ys, sts, decs = [], [], []
            for h in (h0, h0 + 1):
                seg = cum[:, h:h + 1] - cum_t[h:h + 1, :]
                dec = jnp.where(causal, jnp.exp(jnp.minimum(seg, 0.0)), 0.0)
                gmat = cb * dec * dt_t[h:h + 1, :]
                cs_h = cg * ecum[:, h:h + 1]
                lhs = jnp.concatenate([gmat.astype(BF16), cs_h.astype(BF16)], axis=1)
                ys.append(_dot(lhs, rhs))
                bw_t = (bg_t * w_t[h:h + 1, :]).astype(BF16)
                sts.append(_dot(bw_t, xs_pair))
                decs.append(edec_t[h:h + 1, :])
            ybuf[:, h0 * P:(h0 + 2) * P] = jnp.where(lane_lo, ys[0], ys[1])
            dec_pair = jnp.where(lane_lo, decs[0], decs[1])
            state[pair] = dec_pair * st_pair + jnp.where(lane_lo, sts[0], sts[1])

    gw = d_inner // SSM_N_GROUPS
    for g in range(SSM_N_GROUPS):
        cs = slice(g * gw, (g + 1) * gw)
        y = ybuf[:, cs] + dsk_ref[:, cs] * xc[:, cs]
        y = y * _silu(z_ref[:, cs].astype(F32))
        ms = jnp.mean(y * y, axis=-1, keepdims=True)
        y_ref[:, cs] = (y * lax.rsqrt(ms + LN_EPS) * ng_ref[:, cs]).astype(y_ref.dtype)


def _ssd(z, xbc, dt, conv_w, conv_b, dt_bias, a_log, d_skip, norm_g, *, batch, seq, n_heads):
    t, d_inner = z.shape
    conv_dim = xbc.shape[1]
    gn = (conv_dim - d_inner) // 2
    L = SSM_CHUNK
    nc = seq // L
    row = lambda n: pl.BlockSpec((L, n), lambda b, c: (b * nc + c, 0))
    full = lambda a: pl.BlockSpec(a.shape, lambda b, c: (0, 0))
    kern = functools.partial(_ssd_kernel, n_heads=n_heads, d_inner=d_inner, gn=gn)
    return pl.pallas_call(
        kern,
        grid=(batch, nc),
        in_specs=[row(d_inner), row(conv_dim), row(LANES), full(conv_w), full(conv_b),
                  full(dt_bias), full(a_log), full(d_skip), full(norm_g)],
        out_specs=row(d_inner),
        out_shape=jax.ShapeDtypeStruct((t, d_inner), BF16),
        scratch_shapes=[pltpu.VMEM((L + 2 * SUBLANES, conv_dim), F32),
                        pltpu.VMEM((L, conv_dim), F32),
                        pltpu.VMEM((L, d_inner), F32),
                        pltpu.VMEM((n_heads // 2, SSM_D_STATE, 2 * SSM_HEAD_DIM), F32)],
        compiler_params=_params("arbitrary", "arbitrary"),
        name="ssm_conv_ssd",
    )(z, xbc, dt, conv_w, conv_b, dt_bias, a_log, d_skip, norm_g)


def _proj_ln_kernel(x_ref, w_ref, res_ref, g_ref, b_ref, of_ref, ob_ref):
    mix = _dot(x_ref[...], w_ref[...])
    out = _layer_norm(DN_ALPHA * res_ref[...] + mix, g_ref[...], b_ref[...])
    of_ref[...] = out
    ob_ref[...] = out.astype(BF16)


def _proj_ln(xb, w, res, g, b):
    t, k = xb.shape
    d = w.shape[1]
    tm = TOKEN_TILE
    row = lambda n: pl.BlockSpec((tm, n), lambda i: (i, 0))
    full = lambda a: pl.BlockSpec(a.shape, lambda i: (0, 0))
    return pl.pallas_call(
        _proj_ln_kernel,
        grid=(t // tm,),
        in_specs=[row(k), full(w), row(d), full(g), full(b)],
        out_specs=[row(d), row(d)],
        out_shape=[jax.ShapeDtypeStruct((t, d), F32), jax.ShapeDtypeStruct((t, d), BF16)],
        compiler_params=_params("arbitrary"),
        name="proj_residual_ln",
    )(xb, w, res, g, b)


def _ffn_body(x_ref, wg_ref, wu_ref, wd_ref, acc_ref):
    j = pl.program_id(1)

    @pl.when(j == 0)
    def _():
        acc_ref[...] = jnp.zeros(acc_ref.shape, F32)

    x = x_ref[...]
    gate = _dot(x, wg_ref[0])
    up = _dot(x, wu_ref[0])
    hid = (_silu(gate) * up).astype(BF16)
    acc_ref[...] += _dot(hid, wd_ref[0])


def _ffn_dense_kernel(te_ref, nt_ref, x_ref, wg_ref, wu_ref, wd_ref, res_ref, g_ref, b_ref,
                      of_ref, ob_ref, acc_ref):
    _ffn_body(x_ref, wg_ref, wu_ref, wd_ref, acc_ref)

    @pl.when(pl.program_id(1) == pl.num_programs(1) - 1)
    def _():
        out = _layer_norm(DN_ALPHA * res_ref[...] + acc_ref[...], g_ref[...], b_ref[...])
        of_ref[...] = out
        ob_ref[...] = out.astype(BF16)


def _ffn_moe_kernel(te_ref, nt_ref, x_ref, wg_ref, wu_ref, wd_ref, gate_ref, o_ref, acc_ref):
    used = pl.program_id(0) < nt_ref[0]
    last = pl.program_id(1) == pl.num_programs(1) - 1

    @pl.when(used)
    def _():
        _ffn_body(x_ref, wg_ref, wu_ref, wd_ref, acc_ref)

    @pl.when(jnp.logical_and(used, last))
    def _():
        o_ref[...] = acc_ref[...] * gate_ref[...]

    @pl.when(jnp.logical_and(jnp.logical_not(used), last))
    def _():
        o_ref[...] = jnp.zeros(o_ref.shape, F32)


def _ffn_specs(d, dff, tm):
    tf = FF_TILE if dff % FF_TILE == 0 else dff
    nj = dff // tf
    x_spec = pl.BlockSpec((tm, d), lambda i, j, te, nt: (i, 0))
    wgu_spec = pl.BlockSpec((1, d, tf), lambda i, j, te, nt: (te[i], 0, j))
    wd_spec = pl.BlockSpec((1, tf, d), lambda i, j, te, nt: (te[i], j, 0))
    return nj, x_spec, wgu_spec, wd_spec


def _ffn_dense(xb, wg, wu, wd, res, g, b):
    t, d = xb.shape
    tm = TOKEN_TILE
    nt = t // tm
    nj, x_spec, wgu_spec, wd_spec = _ffn_specs(d, wg.shape[-1], tm)
    row = pl.BlockSpec((tm, d), lambda i, j, te, n: (i, 0))
    vec = pl.BlockSpec((1, d), lambda i, j, te, n: (0, 0))
    te = jnp.zeros((nt,), jnp.int32)
    n_used = jnp.full((1,), nt, jnp.int32)
    return pl.pallas_call(
        _ffn_dense_kernel,
        grid_spec=pltpu.PrefetchScalarGridSpec(
            num_scalar_prefetch=2, grid=(nt, nj),
            in_specs=[x_spec, wgu_spec, wgu_spec, wd_spec, row, vec, vec],
            out_specs=[row, row],
            scratch_shapes=[pltpu.VMEM((tm, d), F32)]),
        out_shape=[jax.ShapeDtypeStruct((t, d), F32), jax.ShapeDtypeStruct((t, d), BF16)],
        compiler_params=_params("arbitrary", "arbitrary"),
        name="ffn_dense_ln",
    )(te, n_used, xb, wg[None], wu[None], wd[None], res, g, b)


def _ffn_moe(xs, wg, wu, wd, gate_sorted, tile_expert, n_used):
    p, d = xs.shape
    tm = TOKEN_TILE
    nt = p // tm
    nj, x_spec, wgu_spec, wd_spec = _ffn_specs(d, wg.shape[-1], tm)
    row = pl.BlockSpec((tm, d), lambda i, j, te, n: (i, 0))
    gate_spec = pl.BlockSpec((tm, 1), lambda i, j, te, n: (i, 0))
    return pl.pallas_call(
        _ffn_moe_kernel,
        grid_spec=pltpu.PrefetchScalarGridSpec(
            num_scalar_prefetch=2, grid=(nt, nj),
            in_specs=[x_spec, wgu_spec, wgu_spec, wd_spec, gate_spec],
            out_specs=row,
            scratch_shapes=[pltpu.VMEM((tm, d), F32)]),
        out_shape=jax.ShapeDtypeStruct((p, d), F32),
        compiler_params=_params("arbitrary", "arbitrary"),
        name="ffn_moe_grouped",
    )(tile_expert, n_used, xs, wg, wu, wd, gate_sorted)


def _router_kernel(h_ref, w_ref, b_ref, idx_ref, wt_ref, *, n_experts):
    logits = jnp.dot(h_ref[...], w_ref[...], preferred_element_type=F32,
                     precision=lax.Precision.HIGHEST) + b_ref[...]
    lane = _lane_iota(logits.shape)
    neg = jnp.float32(-jnp.inf)
    logits = jnp.where(lane < n_experts, logits, neg)
    m1 = jnp.max(logits, axis=-1, keepdims=True)
    i1 = jnp.min(jnp.where(logits == m1, lane, LANES), axis=-1, keepdims=True)
    rest = jnp.where(lane == i1, neg, logits)
    m2 = jnp.max(rest, axis=-1, keepdims=True)
    i2 = jnp.min(jnp.where(rest == m2, lane, LANES), axis=-1, keepdims=True)
    e2 = jnp.exp(m2 - m1)
    w1 = 1.0 / (1.0 + e2)
    w2 = e2 / (1.0 + e2)
    idx_ref[...] = jnp.where(lane == 0, i1, jnp.where(lane == 1, i2, 0))
    wt_ref[...] = jnp.where(lane == 0, w1, jnp.where(lane == 1, w2, 0.0))


def _router(h, w_pad, b_pad, n_experts):
    t, d = h.shape
    tm = TOKEN_TILE
    row = lambda n: pl.BlockSpec((tm, n), lambda i: (i, 0))
    full = lambda a: pl.BlockSpec(a.shape, lambda i: (0, 0))
    return pl.pallas_call(
        functools.partial(_router_kernel, n_experts=n_experts),
        grid=(t // tm,),
        in_specs=[row(d), full(w_pad), full(b_pad)],
        out_specs=[row(LANES), row(LANES)],
        out_shape=[jax.ShapeDtypeStruct((t, LANES), jnp.int32),
                   jax.ShapeDtypeStruct((t, LANES), F32)],
        compiler_params=_params("arbitrary"),
        name="moe_router",
    )(h, w_pad, b_pad)


def _row_copy(src_hbm, row, dst, r, sem):
    return pltpu.make_async_copy(src_hbm.at[pl.ds(row, 1), :], dst.at[pl.ds(r, 1), :], sem)


def _gather_rows(idx_hbm, tile, src_hbm, dst, idx_smem, idx_sem, row_sem):
    n = dst.shape[0]
    stage = pltpu.make_async_copy(idx_hbm.at[tile], idx_smem, idx_sem)
    stage.start()
    stage.wait()

    def issue(r, carry):
        _row_copy(src_hbm, idx_smem[r], dst, r, row_sem).start()
        return carry

    lax.fori_loop(0, n, issue, 0, unroll=8)

    def drain(r, carry):
        _row_copy(src_hbm, 0, dst, r, row_sem).wait()
        return carry

    lax.fori_loop(0, n, drain, 0, unroll=8)


def _moe_gather_kernel(nt_ref, idx_hbm, h_hbm, o_ref, buf, idx_smem, idx_sem, row_sem):
    i = pl.program_id(0)

    @pl.when(i < nt_ref[0])
    def _():
        _gather_rows(idx_hbm, i, h_hbm, buf, idx_smem, idx_sem, row_sem)
        o_ref[...] = buf[...].astype(BF16)

    @pl.when(i >= nt_ref[0])
    def _():
        o_ref[...] = jnp.zeros(o_ref.shape, BF16)


def _moe_gather(h, src_rows, n_used):
    t, d = h.shape
    nt, tm = src_rows.shape
    return pl.pallas_call(
        _moe_gather_kernel,
        grid_spec=pltpu.PrefetchScalarGridSpec(
            num_scalar_prefetch=1, grid=(nt,),
            in_specs=[pl.BlockSpec(memory_space=pl.ANY), pl.BlockSpec(memory_space=pl.ANY)],
            out_specs=pl.BlockSpec((tm, d), lambda i, n: (i, 0)),
            scratch_shapes=[pltpu.VMEM((tm, d), F32), pltpu.SMEM((tm,), jnp.int32),
                            pltpu.SemaphoreType.DMA, pltpu.SemaphoreType.DMA]),
        out_shape=jax.ShapeDtypeStruct((nt * tm, d), BF16),
        compiler_params=_params("arbitrary"),
        name="moe_gather",
    )(n_used, src_rows, h)


def _moe_combine_kernel(p0_hbm, p1_hbm, ys_hbm, res_ref, g_ref, b_ref, of_ref, ob_ref,
                        buf0, buf1, idx_smem, idx_sem, sem0, sem1):
    i = pl.program_id(0)
    _gather_rows(p0_hbm, i, ys_hbm, buf0, idx_smem, idx_sem, sem0)
    _gather_rows(p1_hbm, i, ys_hbm, buf1, idx_smem, idx_sem, sem1)
    ffn = buf0[...] + buf1[...]
    out = _layer_norm(DN_ALPHA * res_ref[...] + ffn, g_ref[...], b_ref[...])
    of_ref[...] = out
    ob_ref[...] = out.astype(BF16)


def _moe_combine(ys, pos0, pos1, res, g, b):
    t, d = res.shape
    nt, tm = pos0.shape
    row = pl.BlockSpec((tm, d), lambda i: (i, 0))
    vec = pl.BlockSpec((1, d), lambda i: (0, 0))
    anyspec = pl.BlockSpec(memory_space=pl.ANY)
    return pl.pallas_call(
        _moe_combine_kernel,
        grid=(nt,),
        in_specs=[anyspec, anyspec, anyspec, row, vec, vec],
        out_specs=[row, row],
        out_shape=[jax.ShapeDtypeStruct((t, d), F32), jax.ShapeDtypeStruct((t, d), BF16)],
        scratch_shapes=[pltpu.VMEM((tm, d), F32), pltpu.VMEM((tm, d), F32),
                        pltpu.SMEM((tm,), jnp.int32), pltpu.SemaphoreType.DMA,
                        pltpu.SemaphoreType.DMA, pltpu.SemaphoreType.DMA],
        compiler_params=_params("arbitrary"),
        name="moe_combine_ln",
    )(pos0, pos1, ys, res, g, b)


def _moe_dispatch_tables(idx, wts, tm):
    t = idx.shape[0]
    n_slots = TOP_K * t
    p = n_slots + N_EXPERTS * tm
    nt = p // tm
    ef = idx.reshape(-1)
    onehot = (ef[:, None] == jnp.arange(N_EXPERTS, dtype=jnp.int32)[None, :]).astype(jnp.int32)
    csum = jnp.cumsum(onehot, axis=0)
    rank = jnp.sum(csum * onehot, axis=1) - 1
    counts = csum[-1]
    padded = ((counts + tm - 1) // tm) * tm
    ends = jnp.cumsum(padded)
    starts = ends - padded
    dest = starts[ef] + rank
    tok = jnp.arange(n_slots, dtype=jnp.int32) // TOP_K
    src_rows = jnp.zeros((p,), jnp.int32).at[dest].set(tok)
    gate_sorted = jnp.zeros((p,), F32).at[dest].set(wts.reshape(-1))
    n_used = (ends[-1] // tm).astype(jnp.int32)
    tile_start = jnp.arange(nt, dtype=jnp.int32) * tm
    tile_expert = jnp.sum((tile_start[:, None] >= ends[None, :]).astype(jnp.int32), axis=1)
    tile_expert = jnp.minimum(tile_expert, N_EXPERTS - 1).astype(jnp.int32)
    dest2 = dest.reshape(t, TOP_K)
    return (src_rows.reshape(nt, tm), gate_sorted.reshape(p, 1), tile_expert,
            n_used.reshape(1), dest2[:, 0].reshape(t // tm, tm), dest2[:, 1].reshape(t // tm, tm))


def _moe_layer(h, w_router, b_router, wg, wu, wd, g, b):
    t, d = h.shape
    tm = TOKEN_TILE
    w_pad = jnp.zeros((d, LANES), F32).at[:, :N_EXPERTS].set(w_router)
    b_pad = jnp.zeros((1, LANES), F32).at[0, :N_EXPERTS].set(b_router)
    idx, wts = _router(h, w_pad, b_pad, N_EXPERTS)
    src_rows, gate_sorted, tile_expert, n_used, pos0, pos1 = _moe_dispatch_tables(
        idx[:, :TOP_K], wts[:, :TOP_K], tm)
    xs = _moe_gather(h, src_rows, n_used)
    ys = _ffn_moe(xs, wg, wu, wd, gate_sorted, tile_expert, n_used)
    return _moe_combine(ys, pos0, pos1, h, g, b)


def _rope_tables(pos_ref, invf_ref):
    ang = pos_ref[...] * invf_ref[...]
    return jnp.cos(ang), jnp.sin(ang)


def _rope_block(v, cos, sin):
    half = MLA_ROPE // 2
    lane = _lane_iota(v.shape)
    up = pltpu.roll(v, LANES - half, 1)
    down = pltpu.roll(v, half, 1)
    rot = jnp.where(lane < half, -up, down)
    return jnp.where(lane < MLA_ROPE, v * cos + rot * sin, 0.0)


def _rms(v, g):
    return v * lax.rsqrt(jnp.mean(v * v, axis=-1, keepdims=True) + RMS_EPS) * g


def _mla_kv_kernel(h_ref, pos_ref, invf_ref, wc_ref, wr_ref, g_ref, wkn_ref, wv_ref,
                   k_ref, v_ref):
    hb = h_ref[...]
    ckv = _rms(_dot(hb, wc_ref[...]), g_ref[...]).astype(BF16)
    cos, sin = _rope_tables(pos_ref, invf_ref)
    kr = _rope_block(_dot(hb, wr_ref[...]), cos, sin).astype(BF16)
    kn = _dot(ckv, wkn_ref[...]).astype(BF16)
    v_ref[...] = _dot(ckv, wv_ref[...]).astype(BF16)
    for hd in range(MLA_N_HEADS):
        k_ref[:, hd * 2 * LANES: hd * 2 * LANES + LANES] = kn[:, hd * LANES:(hd + 1) * LANES]
        k_ref[:, hd * 2 * LANES + LANES:(hd + 1) * 2 * LANES] = kr


def _mla_q_kernel(h_ref, pos_ref, invf_ref, wa_ref, g_ref, wqn_ref, wqr_ref, q_ref):
    cq = _rms(_dot(h_ref[...], wa_ref[...]), g_ref[...]).astype(BF16)
    cos, sin = _rope_tables(pos_ref, invf_ref)
    qn = _dot(cq, wqn_ref[...]) * MLA_SCALE
    qr = _dot(cq, wqr_ref[...]) * MLA_SCALE
    for hd in range(MLA_N_HEADS):
        q_ref[:, hd * 2 * LANES: hd * 2 * LANES + LANES] = (
            qn[:, hd * LANES:(hd + 1) * LANES].astype(BF16))
        q_ref[:, hd * 2 * LANES + LANES:(hd + 1) * 2 * LANES] = _rope_block(
            qr[:, hd * LANES:(hd + 1) * LANES], cos, sin).astype(BF16)


def _token_call(kernel, name, t, row_inputs, full_inputs, out_widths, order):
    tm = TOKEN_TILE
    row = lambda n: pl.BlockSpec((tm, n), lambda i: (i, 0))
    full = lambda a: pl.BlockSpec(a.shape, lambda i: (0, 0))
    args, specs = [], []
    for kind, a in order:
        args.append(a)
        specs.append(row(a.shape[1]) if kind == "row" else full(a))
    return pl.pallas_call(
        kernel,
        grid=(t // tm,),
        in_specs=specs,
        out_specs=[row(n) for n in out_widths],
        out_shape=[jax.ShapeDtypeStruct((t, n), BF16) for n in out_widths],
        compiler_params=_params("arbitrary"),
        name=name,
    )(*args)


def _flash_kernel(qi_ref, kj_ref, q_ref, k_ref, v_ref, o_ref, m_ref, l_ref, acc_ref):
    s_idx = pl.program_id(2)
    qi = qi_ref[s_idx]
    kj = kj_ref[s_idx]

    @pl.when(kj == 0)
    def _():
        m_ref[...] = jnp.full(m_ref.shape, -jnp.inf, F32)
        l_ref[...] = jnp.zeros(l_ref.shape, F32)
        acc_ref[...] = jnp.zeros(acc_ref.shape, F32)

    def step(masked):
        sc = lax.dot_general(q_ref[...], k_ref[...], (((1,), (1,)), ((), ())),
                             preferred_element_type=F32)
        if masked:
            r = lax.broadcasted_iota(jnp.int32, sc.shape, 0)
            cidx = lax.broadcasted_iota(jnp.int32, sc.shape, 1)
            sc = jnp.where(cidx <= r, sc, -jnp.inf)
        m_old = m_ref[...]
        m_new = jnp.maximum(m_old, jnp.max(sc, axis=-1, keepdims=True))
        alpha = jnp.exp(m_old - m_new)
        p = jnp.exp(sc - m_new)
        l_ref[...] = alpha * l_ref[...] + jnp.sum(p, axis=-1, keepdims=True)
        acc_ref[...] = alpha * acc_ref[...] + _dot(p.astype(BF16), v_ref[...])
        m_ref[...] = m_new

    @pl.when(kj < qi)
    def _():
        step(False)

    @pl.when(kj == qi)
    def _():
        step(True)
        o_ref[...] = (acc_ref[...] / l_ref[...]).astype(o_ref.dtype)


def _flash_attention(q, k, v, *, batch, seq):
    t = q.shape[0]
    ta = min(ATTN_TILE, seq)
    nq = seq // ta
    pairs = [(i, j) for i in range(nq) for j in range(i + 1)]
    qi = jnp.asarray(np.array([p[0] for p in pairs], np.int32))
    kj = jnp.asarray(np.array([p[1] for p in pairs], np.int32))
    kw = 2 * LANES
    q_spec = pl.BlockSpec((ta, kw), lambda b, h, s, qi, kj: (b * nq + qi[s], h))
    k_spec = pl.BlockSpec((ta, kw), lambda b, h, s, qi, kj: (b * nq + kj[s], h))
    v_spec = pl.BlockSpec((ta, MLA_V), lambda b, h, s, qi, kj: (b * nq + kj[s], h))
    o_spec = pl.BlockSpec((ta, MLA_V), lambda b, h, s, qi, kj: (b * nq + qi[s], h))
    return pl.pallas_call(
        _flash_kernel,
        grid_spec=pltpu.PrefetchScalarGridSpec(
            num_scalar_prefetch=2, grid=(batch, MLA_N_HEADS, len(pairs)),
            in_specs=[q_spec, k_spec, v_spec],
            out_specs=o_spec,
            scratch_shapes=[pltpu.VMEM((ta, 1), F32), pltpu.VMEM((ta, 1), F32),
                            pltpu.VMEM((ta, MLA_V), F32)]),
        out_shape=jax.ShapeDtypeStruct((t, MLA_N_HEADS * MLA_V), BF16),
        compiler_params=_params("arbitrary", "arbitrary", "arbitrary"),
        name="mla_flash_attention",
    )(qi, kj, q, k, v)


def _pad_lanes(w):
    k, nh, r = w.shape
    return jnp.zeros((k, nh, LANES), w.dtype).at[:, :, :r].set(w).reshape(k, nh * LANES)


def kernel(x, positions, ln_g, ln_b, ssm_w_in, ssm_conv_w, ssm_conv_b, ssm_dt_bias, ssm_a_log, ssm_d, ssm_norm_g, ssm_w_out, mla_w_kv_a, mla_kv_norm_g, mla_w_kv_b, mla_w_q_a, mla_q_norm_g, mla_w_q_b, mla_w_o, ffn_w_gate, ffn_w_up, ffn_w_down, moe_w_router, moe_b_router, moe_w_gate, moe_w_up, moe_w_down):
    batch, seq, d = x.shape
    t = batch * seq
    n_a = ssm_w_in.shape[0]
    n_heads = ssm_a_log.shape[1]
    d_inner = n_heads * SSM_HEAD_DIM
    conv_dim = ssm_conv_w.shape[2]
    assert seq % SSM_CHUNK == 0 and t % TOKEN_TILE == 0 and seq % min(ATTN_TILE, seq) == 0
    assert n_heads <= LANES and n_heads % (2 * SSM_N_GROUPS) == 0

    h = x.reshape(t, d)
    hb = h.astype(BF16)
    pos = positions.reshape(t, 1).astype(F32)
    inv_freq = ROPE_BASE ** (-jnp.arange(0, MLA_ROPE, 2, dtype=F32) / MLA_ROPE)
    invf = jnp.tile(inv_freq, LANES // (MLA_ROPE // 2)).reshape(1, LANES)
    vec = lambda a: a.reshape(1, -1)
    pad_heads = lambda a: jnp.zeros((1, LANES), F32).at[0, :n_heads].set(a)

    k_cat = v_all = None
    for layer in range(DEPTH):
        g0, b0 = vec(ln_g[layer, 0]), vec(ln_b[layer, 0])
        g1, b1 = vec(ln_g[layer, 1]), vec(ln_b[layer, 1])
        if layer < n_a:
            a = layer
            w_in = ssm_w_in[a].astype(BF16)
            wz = w_in[:, :d_inner]
            wx = w_in[:, d_inner:d_inner + conv_dim]
            wdt = jnp.zeros((d, LANES), BF16).at[:, :n_heads].set(w_in[:, d_inner + conv_dim:])
            z, xbc, dt = _inproj(hb, wz, wx, wdt)
            y = _ssd(z, xbc, dt, ssm_conv_w[a], vec(ssm_conv_b[a]), pad_heads(ssm_dt_bias[a]),
                     pad_heads(ssm_a_log[a]), vec(jnp.repeat(ssm_d[a], SSM_HEAD_DIM)),
                     vec(ssm_norm_g[a]), batch=batch, seq=seq, n_heads=n_heads)
            h, hb = _proj_ln(y, ssm_w_out[a].astype(BF16), h, g0, b0)
        else:
            bi = layer - n_a
            if bi == 0:
                kv_lora = mla_kv_norm_g.shape[0]
                wc = mla_w_kv_a[:, :kv_lora].astype(BF16)
                wr = _pad_lanes(mla_w_kv_a[:, kv_lora:].reshape(d, 1, MLA_ROPE)).astype(BF16)
                wkv = mla_w_kv_b.reshape(kv_lora, MLA_N_HEADS, MLA_NOPE + MLA_V)
                wkn = wkv[:, :, :MLA_NOPE].reshape(kv_lora, -1).astype(BF16)
                wv = wkv[:, :, MLA_NOPE:].reshape(kv_lora, -1).astype(BF16)
                k_cat, v_all = _token_call(
                    _mla_kv_kernel, "mla_shared_kv", t, None, None,
                    [MLA_N_HEADS * 2 * LANES, MLA_N_HEADS * MLA_V],
                    [("row", hb), ("row", pos), ("full", invf), ("full", wc), ("full", wr),
                     ("full", vec(mla_kv_norm_g)), ("full", wkn), ("full", wv)])
            q_lora = mla_q_norm_g.shape[1]
            wqb = mla_w_q_b[bi].reshape(q_lora, MLA_N_HEADS, MLA_QK)
            wqn = wqb[:, :, :MLA_NOPE].reshape(q_lora, -1).astype(BF16)
            wqr = _pad_lanes(wqb[:, :, MLA_NOPE:]).astype(BF16)
            (q_cat,) = _token_call(
                _mla_q_kernel, "mla_q_proj", t, None, None, [MLA_N_HEADS * 2 * LANES],
                [("row", hb), ("row", pos), ("full", invf), ("full", mla_w_q_a[bi].astype(BF16)),
                 ("full", vec(mla_q_norm_g[bi])), ("full", wqn), ("full", wqr)])
            o = _flash_attention(q_cat, k_cat, v_all, batch=batch, seq=seq)
            h, hb = _proj_ln(o, mla_w_o[bi].astype(BF16), h, g0, b0)
        ci = layer // 2
        if layer % 2 == 0:
            h, hb = _ffn_dense(hb, ffn_w_gate[ci].astype(BF16), ffn_w_up[ci].astype(BF16),
                               ffn_w_down[ci].astype(BF16), h, g1, b1)
        else:
            h, hb = _moe_layer(h, moe_w_router[ci], moe_b_router[ci],
                               moe_w_gate[ci].astype(BF16), moe_w_up[ci].astype(BF16),
                               moe_w_down[ci].astype(BF16), g1, b1)
    return h.reshape(batch, seq, d)
```

```python
import functools
import math

import numpy as np
import jax
import jax.numpy as jnp
from jax import lax
from jax.experimental import pallas as pl
from jax.experimental.pallas import tpu as pltpu

F32 = jnp.float32
BF16 = jnp.bfloat16

DEPTH = 4
DN_ALPHA = (2.0 * DEPTH) ** 0.25
LN_EPS = 1e-5
RMS_EPS = 1e-6
SSM_HEAD_DIM = 64
SSM_N_GROUPS = 4
SSM_D_STATE = 128
SSM_CHUNK = 128
SSM_CONV = 4
MLA_N_HEADS = 8
MLA_NOPE = 128
MLA_ROPE = 64
MLA_V = 128
MLA_QK = MLA_NOPE + MLA_ROPE
MLA_SCALE = MLA_QK ** -0.5
ROPE_BASE = 10000.0
N_EXPERTS = 8
TOP_K = 2
LOG2_E = math.log2(math.e)

LANES = 128
SUBLANES = 8
VMEM_LIMIT = 56 * 1024 * 1024

TOKEN_TILE = 512
FF_TILE = 1408
ATTN_TILE = 2048
ATTN_CHAIN = 512


def _params(*sem):
    return pltpu.CompilerParams(dimension_semantics=sem, vmem_limit_bytes=VMEM_LIMIT)


def _dot(a, b):
    return jnp.dot(a, b, preferred_element_type=F32)


def _silu(v):
    return v * (1.0 / (1.0 + jnp.exp(-v)))


def _layer_norm(v, g, b):
    mu = jnp.mean(v, axis=-1, keepdims=True)
    vc = v - mu
    var = jnp.mean(vc * vc, axis=-1, keepdims=True)
    return vc * lax.rsqrt(var + LN_EPS) * g + b


def _lane_iota(shape):
    return lax.broadcasted_iota(jnp.int32, shape, len(shape) - 1)


def _inproj_kernel(x_ref, wz_ref, wx_ref, wdt_ref, z_ref, xbc_ref, dt_ref):
    x = x_ref[...]
    z_ref[...] = _dot(x, wz_ref[...]).astype(z_ref.dtype)
    xbc_ref[...] = _dot(x, wx_ref[...]).astype(xbc_ref.dtype)
    dt_ref[...] = _dot(x, wdt_ref[...])


def _inproj(xb, wz, wx, wdt):
    t, d = xb.shape
    tm = TOKEN_TILE
    full = lambda w: pl.BlockSpec(w.shape, lambda i: (0, 0))
    row = lambda n: pl.BlockSpec((tm, n), lambda i: (i, 0))
    return pl.pallas_call(
        _inproj_kernel,
        grid=(t // tm,),
        in_specs=[row(d), full(wz), full(wx), full(wdt)],
        out_specs=[row(wz.shape[1]), row(wx.shape[1]), row(wdt.shape[1])],
        out_shape=[jax.ShapeDtypeStruct((t, wz.shape[1]), BF16),
                   jax.ShapeDtypeStruct((t, wx.shape[1]), BF16),
                   jax.ShapeDtypeStruct((t, wdt.shape[1]), F32)],
        compiler_params=_params("arbitrary"),
        name="ssm_in_proj",
    )(xb, wz, wx, wdt)


def _ssd_kernel(z_ref, xbc_ref, dt_ref, cw_ref, cb_ref, dtb_ref, alog_ref, dsk_ref, ng_ref,
                y_ref, ubuf, xc, ybuf, state, *, n_heads, d_inner, gn):
    L = SSM_CHUNK
    P = SSM_HEAD_DIM
    n_state = SSM_D_STATE
    conv_dim = d_inner + 2 * gn
    halo = SUBLANES
    c = pl.program_id(1)

    @pl.when(c == 0)
    def _():
        ubuf[0:halo, :] = jnp.zeros((halo, conv_dim), F32)
        state[...] = jnp.zeros(state.shape, F32)

    ubuf[halo:halo + L, :] = xbc_ref[...].astype(F32)
    cblk = 512
    for j in range(conv_dim // cblk):
        cs = slice(j * cblk, (j + 1) * cblk)
        acc = cb_ref[:, cs] + cw_ref[3:4, cs] * ubuf[halo:halo + L, cs]
        for k in range(SSM_CONV - 1):
            sh = SSM_CONV - 1 - k
            acc = acc + cw_ref[k:k + 1, cs] * ubuf[halo - sh:halo - sh + L, cs]
        xc[:, cs] = _silu(acc)
    ubuf[0:halo, :] = ubuf[L:L + halo, :]

    lane = _lane_iota((L, LANES))
    head_ok = lane < n_heads
    dtr = dt_ref[...] + dtb_ref[...]
    dtp = jnp.maximum(dtr, 0.0) + jnp.log1p(jnp.exp(-jnp.abs(dtr)))
    dtp = jnp.where(head_ok, dtp, 0.0)
    a_row = jnp.where(head_ok[0:1, :], -jnp.exp(alog_ref[...]), 0.0)
    la = dtp * a_row
    row_i = lax.broadcasted_iota(jnp.int32, (L, L), 0)
    col_i = lax.broadcasted_iota(jnp.int32, (L, L), 1)
    causal = row_i >= col_i
    tril = jnp.where(causal, 1.0, 0.0).astype(F32)
    cum = jnp.dot(tril, la, preferred_element_type=F32, precision=lax.Precision.HIGHEST)
    cum_t = cum.T
    dt_t = dtp.T
    ecum = jnp.exp(cum)
    last_t = cum_t[:, L - 1:L]
    w_t = jnp.exp(last_t - cum_t) * dt_t
    edec_t = jnp.exp(last_t)
    lane_lo = _lane_iota((L, 2 * P)) < P

    b_off = d_inner
    c_off = d_inner + gn
    heads_per_group = n_heads // SSM_N_GROUPS
    for g in range(SSM_N_GROUPS):
        bg = xc[:, b_off + g * n_state: b_off + (g + 1) * n_state]
        cg = xc[:, c_off + g * n_state: c_off + (g + 1) * n_state]
        bg_t = bg.T
        cb = _dot(cg.astype(BF16), bg_t.astype(BF16))
        for pr in range(heads_per_group // 2):
            h0 = g * heads_per_group + 2 * pr
            pair = h0 // 2
            xs_pair = xc[:, h0 * P:(h0 + 2) * P].astype(BF16)
            st_pair = state[pair]
            rhs = jnp.concatenate([xs_pair, st_pair.astype(BF16)], axis=0)
            ys, sts, decs = [], [], []
            for h in (h0, h0 + 1):
                seg = cum[:, h:h + 1] - cum_t[h:h + 1, :]
                dec = jnp.where(causal, jnp.exp(jnp.minimum(seg, 0.0)), 0.0)
                gmat = cb * dec * dt_t[h:h + 1, :]
                cs_h = cg * ecum[:, h:h + 1]
                lhs = jnp.concatenate([gmat.astype(BF16), cs_h.astype(BF16)], axis=1)
                ys.append(_dot(lhs, rhs))
                bw_t = (bg_t * w_t[h:h + 1, :]).astype(BF16)
                sts.append(_dot(bw_t, xs_pair))
                decs.append(edec_t[h:h + 1, :])
            ybuf[:, h0 * P:(h0 + 2) * P] = jnp.where(lane_lo, ys[0], ys[1])
            dec_pair = jnp.where(lane_lo, decs[0], decs[1])
            state[pair] = dec_pair * st_pair + jnp.where(lane_lo, sts[0], sts[1])

    gw = d_inner // SSM_N_GROUPS
    for g in range(SSM_N_GROUPS):
        cs = slice(g * gw, (g + 1) * gw)
        y = ybuf[:, cs] + dsk_ref[:, cs] * xc[:, cs]
        y = y * _silu(z_ref[:, cs].astype(F32))
        ms = jnp.mean(y * y, axis=-1, keepdims=True)
        y_ref[:, cs] = (y * lax.rsqrt(ms + LN_EPS) * ng_ref[:, cs]).astype(y_ref.dtype)


def _ssd(z, xbc, dt, conv_w, conv_b, dt_bias, a_log, d_skip, norm_g, *, batch, seq, n_heads):
    t, d_inner = z.shape
    conv_dim = xbc.shape[1]
    gn = (conv_dim - d_inner) // 2
    L = SSM_CHUNK
    nc = seq // L
    row = lambda n: pl.BlockSpec((L, n), lambda b, c: (b * nc + c, 0))
    full = lambda a: pl.BlockSpec(a.shape, lambda b, c: (0, 0))
    kern = functools.partial(_ssd_kernel, n_heads=n_heads, d_inner=d_inner, gn=gn)
    return pl.pallas_call(
        kern,
        grid=(batch, nc),
        in_specs=[row(d_inner), row(conv_dim), row(LANES), full(conv_w), full(conv_b),
                  full(dt_bias), full(a_log), full(d_skip), full(norm_g)],
        out_specs=row(d_inner),
        out_shape=jax.ShapeDtypeStruct((t, d_inner), BF16),
        scratch_shapes=[pltpu.VMEM((L + 2 * SUBLANES, conv_dim), F32),
                        pltpu.VMEM((L, conv_dim), F32),
                        pltpu.VMEM((L, d_inner), F32),
                        pltpu.VMEM((n_heads // 2, SSM_D_STATE, 2 * SSM_HEAD_DIM), F32)],
        compiler_params=_params("arbitrary", "arbitrary"),
        name="ssm_conv_ssd",
    )(z, xbc, dt, conv_w, conv_b, dt_bias, a_log, d_skip, norm_g)


def _proj_ln_kernel(x_ref, w_ref, res_ref, g_ref, b_ref, of_ref, ob_ref):
    mix = _dot(x_ref[...], w_ref[...])
    out = _layer_norm(DN_ALPHA * res_ref[...] + mix, g_ref[...], b_ref[...])
    of_ref[...] = out
    ob_ref[...] = out.astype(BF16)


def _proj_ln(xb, w, res, g, b):
    t, k = xb.shape
    d = w.shape[1]
    tm = TOKEN_TILE
    row = lambda n: pl.BlockSpec((tm, n), lambda i: (i, 0))
    full = lambda a: pl.BlockSpec(a.shape, lambda i: (0, 0))
    return pl.pallas_call(
        _proj_ln_kernel,
        grid=(t // tm,),
        in_specs=[row(k), full(w), row(d), full(g), full(b)],
        out_specs=[row(d), row(d)],
        out_shape=[jax.ShapeDtypeStruct((t, d), F32), jax.ShapeDtypeStruct((t, d), BF16)],
        compiler_params=_params("arbitrary"),
        name="proj_residual_ln",
    )(xb, w, res, g, b)


def _ffn_accumulate(x, wg_ref, wu_ref, wd_ref, acc_ref):
    @pl.when(pl.program_id(1) == 0)
    def _():
        acc_ref[...] = jnp.zeros(acc_ref.shape, F32)

    gate = _dot(x, wg_ref[0])
    up = _dot(x, wu_ref[0])
    hid = (_silu(gate) * up).astype(BF16)
    acc_ref[...] += _dot(hid, wd_ref[0])


def _ffn_dense_kernel(te_ref, nt_ref, x_ref, wg_ref, wu_ref, wd_ref, res_ref, g_ref, b_ref,
                      of_ref, ob_ref, acc_ref):
    _ffn_accumulate(x_ref[...], wg_ref, wu_ref, wd_ref, acc_ref)

    @pl.when(pl.program_id(1) == pl.num_programs(1) - 1)
    def _():
        out = _layer_norm(DN_ALPHA * res_ref[...] + acc_ref[...], g_ref[...], b_ref[...])
        of_ref[...] = out
        ob_ref[...] = out.astype(BF16)


def _ffn_moe_kernel(te_ref, nt_ref, x_ref, wg_ref, wu_ref, wd_ref, o_ref, acc_ref):
    used = pl.program_id(0) < nt_ref[0]
    last = pl.program_id(1) == pl.num_programs(1) - 1

    @pl.when(used)
    def _():
        _ffn_accumulate(x_ref[...].astype(BF16), wg_ref, wu_ref, wd_ref, acc_ref)

    @pl.when(jnp.logical_and(used, last))
    def _():
        o_ref[...] = acc_ref[...]

    @pl.when(jnp.logical_and(jnp.logical_not(used), last))
    def _():
        o_ref[...] = jnp.zeros(o_ref.shape, F32)


def _ffn_specs(d, dff, tm):
    tf = FF_TILE if dff % FF_TILE == 0 else dff
    nj = dff // tf
    x_spec = pl.BlockSpec((tm, d), lambda i, j, te, nt: (i, 0))
    wgu_spec = pl.BlockSpec((1, d, tf), lambda i, j, te, nt: (te[i], 0, j))
    wd_spec = pl.BlockSpec((1, tf, d), lambda i, j, te, nt: (te[i], j, 0))
    return nj, x_spec, wgu_spec, wd_spec


def _ffn_dense(xb, wg, wu, wd, res, g, b):
    t, d = xb.shape
    tm = TOKEN_TILE
    nt = t // tm
    nj, x_spec, wgu_spec, wd_spec = _ffn_specs(d, wg.shape[-1], tm)
    row = pl.BlockSpec((tm, d), lambda i, j, te, n: (i, 0))
    vec = pl.BlockSpec((1, d), lambda i, j, te, n: (0, 0))
    te = jnp.zeros((nt,), jnp.int32)
    n_used = jnp.full((1,), nt, jnp.int32)
    return pl.pallas_call(
        _ffn_dense_kernel,
        grid_spec=pltpu.PrefetchScalarGridSpec(
            num_scalar_prefetch=2, grid=(nt, nj),
            in_specs=[x_spec, wgu_spec, wgu_spec, wd_spec, row, vec, vec],
            out_specs=[row, row],
            scratch_shapes=[pltpu.VMEM((tm, d), F32)]),
        out_shape=[jax.ShapeDtypeStruct((t, d), F32), jax.ShapeDtypeStruct((t, d), BF16)],
        compiler_params=_params("arbitrary", "arbitrary"),
        name="ffn_dense_ln",
    )(te, n_used, xb, wg[None], wu[None], wd[None], res, g, b)


def _ffn_moe(xs, wg, wu, wd, tile_expert, n_used):
    p, d = xs.shape
    tm = TOKEN_TILE
    nt = p // tm
    nj, x_spec, wgu_spec, wd_spec = _ffn_specs(d, wg.shape[-1], tm)
    row = pl.BlockSpec((tm, d), lambda i, j, te, n: (i, 0))
    return pl.pallas_call(
        _ffn_moe_kernel,
        grid_spec=pltpu.PrefetchScalarGridSpec(
            num_scalar_prefetch=2, grid=(nt, nj),
            in_specs=[x_spec, wgu_spec, wgu_spec, wd_spec],
            out_specs=row,
            scratch_shapes=[pltpu.VMEM((tm, d), F32)]),
        out_shape=jax.ShapeDtypeStruct((p, d), F32),
        compiler_params=_params("arbitrary", "arbitrary"),
        name="ffn_moe_grouped",
    )(tile_expert, n_used, xs, wg, wu, wd)


def _router_kernel(h_ref, w_ref, b_ref, idx_ref, wt_ref, *, n_experts):
    logits = jnp.dot(h_ref[...], w_ref[...], preferred_element_type=F32,
                     precision=lax.Precision.HIGHEST) + b_ref[...]
    lane = _lane_iota(logits.shape)
    neg = jnp.float32(-jnp.inf)
    logits = jnp.where(lane < n_experts, logits, neg)
    m1 = jnp.max(logits, axis=-1, keepdims=True)
    i1 = jnp.min(jnp.where(logits == m1, lane, LANES), axis=-1, keepdims=True)
    rest = jnp.where(lane == i1, neg, logits)
    m2 = jnp.max(rest, axis=-1, keepdims=True)
    i2 = jnp.min(jnp.where(rest == m2, lane, LANES), axis=-1, keepdims=True)
    e2 = jnp.exp(m2 - m1)
    w1 = 1.0 / (1.0 + e2)
    w2 = e2 / (1.0 + e2)
    idx_ref[...] = jnp.where(lane == 0, i1, jnp.where(lane == 1, i2, 0))
    wt_ref[...] = jnp.where(lane == 0, w1, jnp.where(lane == 1, w2, 0.0))


def _router(h, w_pad, b_pad, n_experts):
    t, d = h.shape
    tm = TOKEN_TILE
    row = lambda n: pl.BlockSpec((tm, n), lambda i: (i, 0))
    full = lambda a: pl.BlockSpec(a.shape, lambda i: (0, 0))
    return pl.pallas_call(
        functools.partial(_router_kernel, n_experts=n_experts),
        grid=(t // tm,),
        in_specs=[row(d), full(w_pad), full(b_pad)],
        out_specs=[row(LANES), row(LANES)],
        out_shape=[jax.ShapeDtypeStruct((t, LANES), jnp.int32),
                   jax.ShapeDtypeStruct((t, LANES), F32)],
        compiler_params=_params("arbitrary"),
        name="moe_router",
    )(h, w_pad, b_pad)


def _stage_slots(slots_hbm, tile, slots_smem, sem):
    stage = pltpu.make_async_copy(slots_hbm.at[tile], slots_smem, sem)
    stage.start()
    stage.wait()


def _dispatch_copy(h_ref, xs_hbm, r, dst_row, sem):
    return pltpu.make_async_copy(h_ref.at[pl.ds(r, 1), :], xs_hbm.at[pl.ds(dst_row, 1), :], sem)


def _moe_dispatch_kernel(slots_hbm, h_ref, xs_in, xs_hbm, slots_smem, slot_sem, row_sem):
    del xs_in
    tm = h_ref.shape[0]
    _stage_slots(slots_hbm, pl.program_id(0), slots_smem, slot_sem)

    def issue(r, carry):
        _dispatch_copy(h_ref, xs_hbm, r, slots_smem[r], row_sem).start()
        _dispatch_copy(h_ref, xs_hbm, r, slots_smem[tm + r], row_sem).start()
        return carry

    lax.fori_loop(0, tm, issue, 0, unroll=8)

    def drain(r, carry):
        _dispatch_copy(h_ref, xs_hbm, r, 0, row_sem).wait()
        _dispatch_copy(h_ref, xs_hbm, r, 0, row_sem).wait()
        return carry

    lax.fori_loop(0, tm, drain, 0, unroll=8)


def _moe_dispatch(h, slots, n_rows):
    t, d = h.shape
    nt, two_tm = slots.shape
    tm = two_tm // TOP_K
    anyspec = pl.BlockSpec(memory_space=pl.ANY)
    return pl.pallas_call(
        _moe_dispatch_kernel,
        grid=(nt,),
        in_specs=[anyspec, pl.BlockSpec((tm, d), lambda i: (i, 0)), anyspec],
        out_specs=anyspec,
        out_shape=jax.ShapeDtypeStruct((n_rows, d), F32),
        scratch_shapes=[pltpu.SMEM((two_tm,), jnp.int32), pltpu.SemaphoreType.DMA,
                        pltpu.SemaphoreType.DMA],
        input_output_aliases={2: 0},
        compiler_params=_params("arbitrary"),
        name="moe_dispatch",
    )(slots, h, jnp.zeros((n_rows, d), F32))


def _combine_copy(ys_hbm, src_row, buf, slot, r, sems):
    return pltpu.make_async_copy(ys_hbm.at[pl.ds(src_row, 1), :],
                                 buf.at[slot, pl.ds(r, 1), :], sems.at[slot])


def _moe_combine_kernel(slots_hbm, ys_hbm, wt_ref, res_ref, g_ref, b_ref, of_ref, ob_ref,
                        buf, slots_smem, slot_sem, sems):
    i = pl.program_id(0)
    n = pl.num_programs(0)
    tm = res_ref.shape[0]
    n_rows = TOP_K * tm

    def issue_tile(tile, slot):
        _stage_slots(slots_hbm, tile, slots_smem, slot_sem)

        def issue(r, carry):
            _combine_copy(ys_hbm, slots_smem[r], buf, slot, r, sems).start()
            return carry

        lax.fori_loop(0, n_rows, issue, 0, unroll=8)

    @pl.when(i == 0)
    def _():
        issue_tile(0, 0)

    @pl.when(i + 1 < n)
    def _():
        issue_tile(i + 1, (i + 1) % 2)

    slot = i % 2

    def drain(r, carry):
        _combine_copy(ys_hbm, 0, buf, slot, r, sems).wait()
        return carry

    lax.fori_loop(0, n_rows, drain, 0, unroll=8)

    w0 = wt_ref[:, 0:1]
    w1 = wt_ref[:, 1:2]
    ffn = w0 * buf[slot, 0:tm, :] + w1 * buf[slot, tm:n_rows, :]
    out = _layer_norm(DN_ALPHA * res_ref[...] + ffn, g_ref[...], b_ref[...])
    of_ref[...] = out
    ob_ref[...] = out.astype(BF16)


def _moe_combine(ys, slots, wts, res, g, b):
    t, d = res.shape
    nt, two_tm = slots.shape
    tm = two_tm // TOP_K
    row = lambda n: pl.BlockSpec((tm, n), lambda i: (i, 0))
    vec = pl.BlockSpec((1, d), lambda i: (0, 0))
    anyspec = pl.BlockSpec(memory_space=pl.ANY)
    return pl.pallas_call(
        _moe_combine_kernel,
        grid=(nt,),
        in_specs=[anyspec, anyspec, row(LANES), row(d), vec, vec],
        out_specs=[row(d), row(d)],
        out_shape=[jax.ShapeDtypeStruct((t, d), F32), jax.ShapeDtypeStruct((t, d), BF16)],
        scratch_shapes=[pltpu.VMEM((2, two_tm, d), F32), pltpu.SMEM((two_tm,), jnp.int32),
                        pltpu.SemaphoreType.DMA, pltpu.SemaphoreType.DMA((2,))],
        compiler_params=_params("arbitrary"),
        name="moe_combine_ln",
    )(slots, ys, wts, res, g, b)


def _moe_slot_tables(idx, tm):
    t = idx.shape[0]
    n_slots = TOP_K * t
    n_rows = n_slots + N_EXPERTS * tm
    nt = n_rows // tm
    ef = idx.reshape(-1)
    onehot = (ef[:, None] == jnp.arange(N_EXPERTS, dtype=jnp.int32)[None, :]).astype(jnp.int32)
    csum = jnp.cumsum(onehot, axis=0)
    rank = jnp.sum(csum * onehot, axis=1) - 1
    counts = csum[-1]
    padded = ((counts + tm - 1) // tm) * tm
    ends = jnp.cumsum(padded)
    starts = ends - padded
    dest = jnp.sum(starts[None, :] * onehot, axis=1) + rank
    n_used = (ends[-1] // tm).astype(jnp.int32).reshape(1)
    tile_start = jnp.arange(nt, dtype=jnp.int32) * tm
    tile_expert = jnp.sum((tile_start[:, None] >= ends[None, :]).astype(jnp.int32), axis=1)
    tile_expert = jnp.minimum(tile_expert, N_EXPERTS - 1).astype(jnp.int32)
    slots = dest.reshape(t // tm, tm, TOP_K).transpose(0, 2, 1).reshape(t // tm, TOP_K * tm)
    return slots.astype(jnp.int32), tile_expert, n_used, n_rows


def _moe_layer(h, w_router, b_router, wg, wu, wd, g, b):
    t, d = h.shape
    tm = TOKEN_TILE
    w_pad = jnp.zeros((d, LANES), F32).at[:, :N_EXPERTS].set(w_router)
    b_pad = jnp.zeros((1, LANES), F32).at[0, :N_EXPERTS].set(b_router)
    idx, wts = _router(h, w_pad, b_pad, N_EXPERTS)
    slots, tile_expert, n_used, n_rows = _moe_slot_tables(idx[:, :TOP_K], tm)
    xs = _moe_dispatch(h, slots, n_rows)
    ys = _ffn_moe(xs, wg, wu, wd, tile_expert, n_used)
    return _moe_combine(ys, slots, wts, h, g, b)


def _rope_tables(pos_ref, invf_ref):
    ang = pos_ref[...] * invf_ref[...]
    return jnp.cos(ang), jnp.sin(ang)


def _rope_block(v, cos, sin):
    half = MLA_ROPE // 2
    lane = _lane_iota(v.shape)
    up = pltpu.roll(v, LANES - half, 1)
    down = pltpu.roll(v, half, 1)
    rot = jnp.where(lane < half, -up, down)
    return jnp.where(lane < MLA_ROPE, v * cos + rot * sin, 0.0)


def _rms(v, g):
    return v * lax.rsqrt(jnp.mean(v * v, axis=-1, keepdims=True) + RMS_EPS) * g


def _mla_kv_kernel(h_ref, pos_ref, invf_ref, wc_ref, wr_ref, g_ref, wkn_ref, wv_ref,
                   k_ref, v_ref):
    hb = h_ref[...]
    ckv = _rms(_dot(hb, wc_ref[...]), g_ref[...]).astype(BF16)
    cos, sin = _rope_tables(pos_ref, invf_ref)
    kr = _rope_block(_dot(hb, wr_ref[...]), cos, sin).astype(BF16)
    kn = _dot(ckv, wkn_ref[...]).astype(BF16)
    vv = _dot(ckv, wv_ref[...]).astype(BF16)
    ones = jnp.ones((hb.shape[0], LANES), BF16)
    for hd in range(MLA_N_HEADS):
        lo, mid, hi = hd * 2 * LANES, hd * 2 * LANES + LANES, (hd + 1) * 2 * LANES
        k_ref[:, lo:mid] = kn[:, hd * LANES:(hd + 1) * LANES]
        k_ref[:, mid:hi] = kr
        v_ref[:, lo:mid] = vv[:, hd * LANES:(hd + 1) * LANES]
        v_ref[:, mid:hi] = ones


def _mla_q_kernel(h_ref, pos_ref, invf_ref, wa_ref, g_ref, wqn_ref, wqr_ref, q_ref):
    cq = _rms(_dot(h_ref[...], wa_ref[...]), g_ref[...]).astype(BF16)
    cos, sin = _rope_tables(pos_ref, invf_ref)
    scale = MLA_SCALE * LOG2_E
    qn = _dot(cq, wqn_ref[...]) * scale
    qr = _dot(cq, wqr_ref[...]) * scale
    for hd in range(MLA_N_HEADS):
        lo, mid, hi = hd * 2 * LANES, hd * 2 * LANES + LANES, (hd + 1) * 2 * LANES
        q_ref[:, lo:mid] = qn[:, hd * LANES:(hd + 1) * LANES].astype(BF16)
        q_ref[:, mid:hi] = _rope_block(qr[:, hd * LANES:(hd + 1) * LANES], cos, sin).astype(BF16)


def _token_call(kernel, name, t, operands, out_widths):
    tm = TOKEN_TILE
    row = lambda n: pl.BlockSpec((tm, n), lambda i: (i, 0))
    full = lambda a: pl.BlockSpec(a.shape, lambda i: (0, 0))
    return pl.pallas_call(
        kernel,
        grid=(t // tm,),
        in_specs=[row(a.shape[1]) if kind == "row" else full(a) for kind, a in operands],
        out_specs=[row(n) for n in out_widths],
        out_shape=[jax.ShapeDtypeStruct((t, n), BF16) for n in out_widths],
        compiler_params=_params("arbitrary"),
        name=name,
    )(*[a for _, a in operands])


def _flash_kernel(qi_ref, kj_ref, q_ref, k_ref, v_ref, o_ref, m_ref, acc_ref, *, chain):
    s_idx = pl.program_id(2)
    qi = qi_ref[s_idx]
    kj = kj_ref[s_idx]
    tq = q_ref.shape[0]
    n_chain = tq // chain

    @pl.when(kj == 0)
    def _():
        m_ref[...] = jnp.full(m_ref.shape, -jnp.inf, F32)
        acc_ref[...] = jnp.zeros(acc_ref.shape, F32)

    def chain_update(r, width, masked):
        rows = slice(r * chain, (r + 1) * chain)
        sc = lax.dot_general(q_ref[rows, :], k_ref[0:width, :], (((1,), (1,)), ((), ())),
                             preferred_element_type=F32)
        if masked:
            ri = lax.broadcasted_iota(jnp.int32, (chain, chain), 0)
            ci = lax.broadcasted_iota(jnp.int32, (chain, chain), 1)
            diag = jnp.where(ci <= ri, sc[:, width - chain:width], -jnp.inf)
            sc = diag if width == chain else jnp.concatenate(
                [sc[:, 0:width - chain], diag], axis=1)
        m_old = m_ref[rows, :]
        m_new = jnp.maximum(m_old, jnp.max(sc, axis=-1, keepdims=True))
        alpha = jnp.exp2(m_old - m_new)
        p = jnp.exp2(sc - m_new).astype(BF16)
        acc_ref[rows, :] = alpha * acc_ref[rows, :] + _dot(p, v_ref[0:width, :])
        m_ref[rows, :] = m_new

    @pl.when(kj < qi)
    def _():
        for r in range(n_chain):
            chain_update(r, k_ref.shape[0], False)

    @pl.when(kj == qi)
    def _():
        for r in range(n_chain):
            chain_update(r, (r + 1) * chain, True)
        o_ref[...] = (acc_ref[:, 0:MLA_V] / acc_ref[:, MLA_V:2 * MLA_V]).astype(o_ref.dtype)


def _flash_attention(q, k, v, *, batch, seq):
    t = q.shape[0]
    ta = min(ATTN_TILE, seq)
    chain = min(ATTN_CHAIN, ta)
    nq = seq // ta
    pairs = [(i, j) for i in range(nq) for j in range(i + 1)]
    qi = jnp.asarray(np.array([p[0] for p in pairs], np.int32))
    kj = jnp.asarray(np.array([p[1] for p in pairs], np.int32))
    kw = 2 * LANES
    q_spec = pl.BlockSpec((ta, kw), lambda b, h, s, qi, kj: (b * nq + qi[s], h))
    kv_spec = pl.BlockSpec((ta, kw), lambda b, h, s, qi, kj: (b * nq + kj[s], h))
    o_spec = pl.BlockSpec((ta, MLA_V), lambda b, h, s, qi, kj: (b * nq + qi[s], h))
    return pl.pallas_call(
        functools.partial(_flash_kernel, chain=chain),
        grid_spec=pltpu.PrefetchScalarGridSpec(
            num_scalar_prefetch=2, grid=(batch, MLA_N_HEADS, len(pairs)),
            in_specs=[q_spec, kv_spec, kv_spec],
            out_specs=o_spec,
            scratch_shapes=[pltpu.VMEM((ta, 1), F32), pltpu.VMEM((ta, kw), F32)]),
        out_shape=jax.ShapeDtypeStruct((t, MLA_N_HEADS * MLA_V), BF16),
        compiler_params=_params("arbitrary", "arbitrary", "arbitrary"),
        name="mla_flash_attention",
    )(qi, kj, q, k, v)


def _pad_lanes(w):
    k, nh, r = w.shape
    return jnp.zeros((k, nh, LANES), w.dtype).at[:, :, :r].set(w).reshape(k, nh * LANES)


def kernel(x, positions, ln_g, ln_b, ssm_w_in, ssm_conv_w, ssm_conv_b, ssm_dt_bias, ssm_a_log, ssm_d, ssm_norm_g, ssm_w_out, mla_w_kv_a, mla_kv_norm_g, mla_w_kv_b, mla_w_q_a, mla_q_norm_g, mla_w_q_b, mla_w_o, ffn_w_gate, ffn_w_up, ffn_w_down, moe_w_router, moe_b_router, moe_w_gate, moe_w_up, moe_w_down):
    batch, seq, d = x.shape
    t = batch * seq
    n_a = ssm_w_in.shape[0]
    n_heads = ssm_a_log.shape[1]
    d_inner = n_heads * SSM_HEAD_DIM
    conv_dim = ssm_conv_w.shape[2]
    ta = min(ATTN_TILE, seq)
    assert seq % SSM_CHUNK == 0 and t % TOKEN_TILE == 0
    assert seq % ta == 0 and ta % min(ATTN_CHAIN, ta) == 0
    assert n_heads <= LANES and n_heads % (2 * SSM_N_GROUPS) == 0

    h = x.reshape(t, d)
    hb = h.astype(BF16)
    pos = positions.reshape(t, 1).astype(F32)
    inv_freq = ROPE_BASE ** (-jnp.arange(0, MLA_ROPE, 2, dtype=F32) / MLA_ROPE)
    invf = jnp.tile(inv_freq, LANES // (MLA_ROPE // 2)).reshape(1, LANES)
    vec = lambda a: a.reshape(1, -1)
    pad_heads = lambda a: jnp.zeros((1, LANES), F32).at[0, :n_heads].set(a)

    k_cat = v_cat = None
    for layer in range(DEPTH):
        g0, b0 = vec(ln_g[layer, 0]), vec(ln_b[layer, 0])
        g1, b1 = vec(ln_g[layer, 1]), vec(ln_b[layer, 1])
        if layer < n_a:
            a = layer
            w_in = ssm_w_in[a].astype(BF16)
            wz = w_in[:, :d_inner]
            wx = w_in[:, d_inner:d_inner + conv_dim]
            wdt = jnp.zeros((d, LANES), BF16).at[:, :n_heads].set(w_in[:, d_inner + conv_dim:])
            z, xbc, dt = _inproj(hb, wz, wx, wdt)
            y = _ssd(z, xbc, dt, ssm_conv_w[a], vec(ssm_conv_b[a]), pad_heads(ssm_dt_bias[a]),
                     pad_heads(ssm_a_log[a]), vec(jnp.repeat(ssm_d[a], SSM_HEAD_DIM)),
                     vec(ssm_norm_g[a]), batch=batch, seq=seq, n_heads=n_heads)
            h, hb = _proj_ln(y, ssm_w_out[a].astype(BF16), h, g0, b0)
        else:
            bi = layer - n_a
            cat_width = MLA_N_HEADS * 2 * LANES
            if bi == 0:
                kv_lora = mla_kv_norm_g.shape[0]
                wc = mla_w_kv_a[:, :kv_lora].astype(BF16)
                wr = _pad_lanes(mla_w_kv_a[:, kv_lora:].reshape(d, 1, MLA_ROPE)).astype(BF16)
                wkv = mla_w_kv_b.reshape(kv_lora, MLA_N_HEADS, MLA_NOPE + MLA_V)
                wkn = wkv[:, :, :MLA_NOPE].reshape(kv_lora, -1).astype(BF16)
                wv = wkv[:, :, MLA_NOPE:].reshape(kv_lora, -1).astype(BF16)
                k_cat, v_cat = _token_call(
                    _mla_kv_kernel, "mla_shared_kv", t,
                    [("row", hb), ("row", pos), ("full", invf), ("full", wc), ("full", wr),
                     ("full", vec(mla_kv_norm_g)), ("full", wkn), ("full", wv)],
                    [cat_width, cat_width])
            q_lora = mla_q_norm_g.shape[1]
            wqb = mla_w_q_b[bi].reshape(q_lora, MLA_N_HEADS, MLA_QK)
            wqn = wqb[:, :, :MLA_NOPE].reshape(q_lora, -1).astype(BF16)
            wqr = _pad_lanes(wqb[:, :, MLA_NOPE:]).astype(BF16)
            (q_cat,) = _token_call(
                _mla_q_kernel, "mla_q_proj", t,
                [("row", hb), ("row", pos), ("full", invf), ("full", mla_w_q_a[bi].astype(BF16)),
                 ("full", vec(mla_q_norm_g[bi])), ("full", wqn), ("full", wqr)],
                [cat_width])
            o = _flash_attention(q_cat, k_cat, v_cat, batch=batch, seq=seq)
            h, hb = _proj_ln(o, mla_w_o[bi].astype(BF16), h, g0, b0)
        ci = layer // 2
        if layer % 2 == 0:
            h, hb = _ffn_dense(hb, ffn_w_gate[ci].astype(BF16), ffn_w_up[ci].astype(BF16),
                               ffn_w_down[ci].astype(BF16), h, g1, b1)
        else:
            h, hb = _moe_layer(h, moe_w_router[ci], moe_b_router[ci],
                               moe_w_gate[ci].astype(BF16), moe_w_up[ci].astype(BF16),
                               moe_w_down[ci].astype(BF16), g1, b1)
    return h.reshape(batch, seq, d)
```

```python
import functools
import math

import numpy as np
import jax
import jax.numpy as jnp
from jax import lax
from jax.experimental import pallas as pl
from jax.experimental.pallas import tpu as pltpu

F32 = jnp.float32
BF16 = jnp.bfloat16

DEPTH = 4
DN_ALPHA = (2.0 * DEPTH) ** 0.25
LN_EPS = 1e-5
RMS_EPS = 1e-6
SSM_HEAD_DIM = 64
SSM_N_GROUPS = 4
SSM_D_STATE = 128
SSM_CHUNK = 128
SSM_CONV = 4
MLA_N_HEADS = 8
MLA_NOPE = 128
MLA_ROPE = 64
MLA_V = 128
MLA_QK = MLA_NOPE + MLA_ROPE
MLA_SCALE = MLA_QK ** -0.5
ROPE_BASE = 10000.0
N_EXPERTS = 8
TOP_K = 2
LOG2_E = math.log2(math.e)

LANES = 128
SUBLANES = 8
VMEM_LIMIT = 56 * 1024 * 1024

TOKEN_TILE = 512
FF_CHUNK = 256
MOE_TOKEN_TILE = 1024
ATTN_TILE = 2048
ATTN_CHAIN = 256


def _params(*sem):
    return pltpu.CompilerParams(dimension_semantics=sem, vmem_limit_bytes=VMEM_LIMIT)


def _dot(a, b):
    return jnp.dot(a, b, preferred_element_type=F32)


def _silu(v):
    return v * (1.0 / (1.0 + jnp.exp(-v)))


def _layer_norm(v, g, b):
    mu = jnp.mean(v, axis=-1, keepdims=True)
    vc = v - mu
    var = jnp.mean(vc * vc, axis=-1, keepdims=True)
    return vc * lax.rsqrt(var + LN_EPS) * g + b


def _lane_iota(shape):
    return lax.broadcasted_iota(jnp.int32, shape, len(shape) - 1)


def _inproj_kernel(x_ref, wz_ref, wx_ref, wdt_ref, z_ref, xbc_ref, dt_ref):
    x = x_ref[...]
    z_ref[...] = _dot(x, wz_ref[...]).astype(z_ref.dtype)
    xbc_ref[...] = _dot(x, wx_ref[...]).astype(xbc_ref.dtype)
    dt_ref[...] = _dot(x, wdt_ref[...])


def _inproj(xb, wz, wx, wdt):
    t, d = xb.shape
    tm = TOKEN_TILE
    full = lambda w: pl.BlockSpec(w.shape, lambda i: (0, 0))
    row = lambda n: pl.BlockSpec((tm, n), lambda i: (i, 0))
    return pl.pallas_call(
        _inproj_kernel,
        grid=(t // tm,),
        in_specs=[row(d), full(wz), full(wx), full(wdt)],
        out_specs=[row(wz.shape[1]), row(wx.shape[1]), row(wdt.shape[1])],
        out_shape=[jax.ShapeDtypeStruct((t, wz.shape[1]), BF16),
                   jax.ShapeDtypeStruct((t, wx.shape[1]), BF16),
                   jax.ShapeDtypeStruct((t, wdt.shape[1]), F32)],
        compiler_params=_params("arbitrary"),
        name="ssm_in_proj",
    )(xb, wz, wx, wdt)


def _ssd_kernel(z_ref, xbc_ref, dt_ref, cw_ref, cb_ref, dtb_ref, alog_ref, dsk_ref, ng_ref,
                y_ref, ubuf, xc, ybuf, state, *, n_heads, d_inner, gn):
    L = SSM_CHUNK
    P = SSM_HEAD_DIM
    n_state = SSM_D_STATE
    conv_dim = d_inner + 2 * gn
    halo = SUBLANES
    c = pl.program_id(1)

    @pl.when(c == 0)
    def _():
        ubuf[0:halo, :] = jnp.zeros((halo, conv_dim), F32)
        state[...] = jnp.zeros(state.shape, F32)

    ubuf[halo:halo + L, :] = xbc_ref[...].astype(F32)
    cblk = 512
    for j in range(conv_dim // cblk):
        cs = slice(j * cblk, (j + 1) * cblk)
        acc = cb_ref[:, cs] + cw_ref[3:4, cs] * ubuf[halo:halo + L, cs]
        for k in range(SSM_CONV - 1):
            sh = SSM_CONV - 1 - k
            acc = acc + cw_ref[k:k + 1, cs] * ubuf[halo - sh:halo - sh + L, cs]
        xc[:, cs] = _silu(acc)
    ubuf[0:halo, :] = ubuf[L:L + halo, :]

    lane = _lane_iota((L, LANES))
    head_ok = lane < n_heads
    dtr = dt_ref[...] + dtb_ref[...]
    dtp = jnp.maximum(dtr, 0.0) + jnp.log1p(jnp.exp(-jnp.abs(dtr)))
    dtp = jnp.where(head_ok, dtp, 0.0)
    a_row = jnp.where(head_ok[0:1, :], -jnp.exp(alog_ref[...]), 0.0)
    la = dtp * a_row
    row_i = lax.broadcasted_iota(jnp.int32, (L, L), 0)
    col_i = lax.broadcasted_iota(jnp.int32, (L, L), 1)
    causal = row_i >= col_i
    tril = jnp.where(causal, 1.0, 0.0).astype(F32)
    cum = jnp.dot(tril, la, preferred_element_type=F32, precision=lax.Precision.HIGHEST)
    cum_t = cum.T
    dt_t = dtp.T
    ecum = jnp.exp(cum)
    last_t = cum_t[:, L - 1:L]
    w_t = jnp.exp(last_t - cum_t) * dt_t
    edec_t = jnp.exp(last_t)
    lane_lo = _lane_iota((L, 2 * P)) < P

    b_off = d_inner
    c_off = d_inner + gn
    heads_per_group = n_heads // SSM_N_GROUPS
    for g in range(SSM_N_GROUPS):
        bg = xc[:, b_off + g * n_state: b_off + (g + 1) * n_state]
        cg = xc[:, c_off + g * n_state: c_off + (g + 1) * n_state]
        bg_t = bg.T
        cb = _dot(cg.astype(BF16), bg_t.astype(BF16))
        for pr in range(heads_per_group // 2):
            h0 = g * heads_per_group + 2 * pr
            pair = h0 // 2
            xs_pair = xc[:, h0 * P:(h0 + 2) * P].astype(BF16)
            st_pair = state[pair]
            rhs = jnp.concatenate([xs_pair, st_pair.astype(BF16)], axis=0)
            ys, sts, decs = [], [], []
            for h in (h0, h0 + 1):
                seg = cum[:, h:h + 1] - cum_t[h:h + 1, :]
                dec = jnp.where(causal, jnp.exp(jnp.minimum(seg, 0.0)), 0.0)
                gmat = cb * dec * dt_t[h:h + 1, :]
                cs_h = cg * ecum[:, h:h + 1]
                lhs = jnp.concatenate([gmat.astype(BF16), cs_h.astype(BF16)], axis=1)
                ys.append(_dot(lhs, rhs))
                bw_t = (bg_t * w_t[h:h + 1, :]).astype(BF16)
                sts.append(_dot(bw_t, xs_pair))
                decs.append(edec_t[h:h + 1, :])
            ybuf[:, h0 * P:(h0 + 2) * P] = jnp.where(lane_lo, ys[0], ys[1])
            dec_pair = jnp.where(lane_lo, decs[0], decs[1])
            state[pair] = dec_pair * st_pair + jnp.where(lane_lo, sts[0], sts[1])

    gw = d_inner // SSM_N_GROUPS
    for g in range(SSM_N_GROUPS):
        cs = slice(g * gw, (g + 1) * gw)
        y = ybuf[:, cs] + dsk_ref[:, cs] * xc[:, cs]
        y = y * _silu(z_ref[:, cs].astype(F32))
        ms = jnp.mean(y * y, axis=-1, keepdims=True)
        y_ref[:, cs] = (y * lax.rsqrt(ms + LN_EPS) * ng_ref[:, cs]).astype(y_ref.dtype)


def _ssd(z, xbc, dt, conv_w, conv_b, dt_bias, a_log, d_skip, norm_g, *, batch, seq, n_heads):
    t, d_inner = z.shape
    conv_dim = xbc.shape[1]
    gn = (conv_dim - d_inner) // 2
    L = SSM_CHUNK
    nc = seq // L
    row = lambda n: pl.BlockSpec((L, n), lambda b, c: (b * nc + c, 0))
    full = lambda a: pl.BlockSpec(a.shape, lambda b, c: (0, 0))
    kern = functools.partial(_ssd_kernel, n_heads=n_heads, d_inner=d_inner, gn=gn)
    return pl.pallas_call(
        kern,
        grid=(batch, nc),
        in_specs=[row(d_inner), row(conv_dim), row(LANES), full(conv_w), full(conv_b),
                  full(dt_bias), full(a_log), full(d_skip), full(norm_g)],
        out_specs=row(d_inner),
        out_shape=jax.ShapeDtypeStruct((t, d_inner), BF16),
        scratch_shapes=[pltpu.VMEM((L + 2 * SUBLANES, conv_dim), F32),
                        pltpu.VMEM((L, conv_dim), F32),
                        pltpu.VMEM((L, d_inner), F32),
                        pltpu.VMEM((n_heads // 2, SSM_D_STATE, 2 * SSM_HEAD_DIM), F32)],
        compiler_params=_params("arbitrary", "arbitrary"),
        name="ssm_conv_ssd",
    )(z, xbc, dt, conv_w, conv_b, dt_bias, a_log, d_skip, norm_g)


def _proj_ln_kernel(x_ref, w_ref, res_ref, g_ref, b_ref, of_ref, ob_ref):
    mix = _dot(x_ref[...], w_ref[...])
    out = _layer_norm(DN_ALPHA * res_ref[...] + mix, g_ref[...], b_ref[...])
    of_ref[...] = out
    ob_ref[...] = out.astype(BF16)


def _proj_ln(xb, w, res, g, b):
    t, k = xb.shape
    d = w.shape[1]
    tm = TOKEN_TILE
    row = lambda n: pl.BlockSpec((tm, n), lambda i: (i, 0))
    full = lambda a: pl.BlockSpec(a.shape, lambda i: (0, 0))
    return pl.pallas_call(
        _proj_ln_kernel,
        grid=(t // tm,),
        in_specs=[row(k), full(w), row(d), full(g), full(b)],
        out_specs=[row(d), row(d)],
        out_shape=[jax.ShapeDtypeStruct((t, d), F32), jax.ShapeDtypeStruct((t, d), BF16)],
        compiler_params=_params("arbitrary"),
        name="proj_residual_ln",
    )(xb, w, res, g, b)


def _ffn_tile(x, wg_ref, wu_ref, wd_ref):
    dff = wg_ref.shape[2]
    acc = None
    for c in range(dff // FF_CHUNK):
        cs = slice(c * FF_CHUNK, (c + 1) * FF_CHUNK)
        gate = _dot(x, wg_ref[0, :, cs])
        up = _dot(x, wu_ref[0, :, cs])
        hid = (_silu(gate) * up).astype(BF16)
        part = _dot(hid, wd_ref[0, cs, :])
        acc = part if acc is None else acc + part
    return acc


def _ffn_dense_kernel(te_ref, nt_ref, x_ref, wg_ref, wu_ref, wd_ref, res_ref, g_ref, b_ref,
                      of_ref, ob_ref):
    ffn = _ffn_tile(x_ref[...], wg_ref, wu_ref, wd_ref)
    out = _layer_norm(DN_ALPHA * res_ref[...] + ffn, g_ref[...], b_ref[...])
    of_ref[...] = out
    ob_ref[...] = out.astype(BF16)


def _ffn_moe_kernel(te_ref, nt_ref, x_ref, wg_ref, wu_ref, wd_ref, o_ref):
    used = pl.program_id(0) < nt_ref[0]

    @pl.when(used)
    def _():
        o_ref[...] = _ffn_tile(x_ref[...].astype(BF16), wg_ref, wu_ref, wd_ref)

    @pl.when(jnp.logical_not(used))
    def _():
        o_ref[...] = jnp.zeros(o_ref.shape, F32)


def _ffn_specs(d, dff, tm):
    assert dff % FF_CHUNK == 0
    x_spec = pl.BlockSpec((tm, d), lambda i, te, nt: (i, 0))
    wgu_spec = pl.BlockSpec((1, d, dff), lambda i, te, nt: (te[i], 0, 0))
    wd_spec = pl.BlockSpec((1, dff, d), lambda i, te, nt: (te[i], 0, 0))
    return x_spec, wgu_spec, wd_spec


def _ffn_dense(xb, wg, wu, wd, res, g, b):
    t, d = xb.shape
    tm = TOKEN_TILE
    nt = t // tm
    x_spec, wgu_spec, wd_spec = _ffn_specs(d, wg.shape[-1], tm)
    vec = pl.BlockSpec((1, d), lambda i, te, n: (0, 0))
    te = jnp.zeros((nt,), jnp.int32)
    n_used = jnp.full((1,), nt, jnp.int32)
    return pl.pallas_call(
        _ffn_dense_kernel,
        grid_spec=pltpu.PrefetchScalarGridSpec(
            num_scalar_prefetch=2, grid=(nt,),
            in_specs=[x_spec, wgu_spec, wgu_spec, wd_spec, x_spec, vec, vec],
            out_specs=[x_spec, x_spec]),
        out_shape=[jax.ShapeDtypeStruct((t, d), F32), jax.ShapeDtypeStruct((t, d), BF16)],
        compiler_params=_params("arbitrary"),
        name="ffn_dense_ln",
    )(te, n_used, xb, wg[None], wu[None], wd[None], res, g, b)


def _ffn_moe(xs, wg, wu, wd, tile_expert, n_used):
    p, d = xs.shape
    tm = TOKEN_TILE
    nt = p // tm
    x_spec, wgu_spec, wd_spec = _ffn_specs(d, wg.shape[-1], tm)
    return pl.pallas_call(
        _ffn_moe_kernel,
        grid_spec=pltpu.PrefetchScalarGridSpec(
            num_scalar_prefetch=2, grid=(nt,),
            in_specs=[x_spec, wgu_spec, wgu_spec, wd_spec],
            out_specs=x_spec),
        out_shape=jax.ShapeDtypeStruct((p, d), F32),
        compiler_params=_params("arbitrary"),
        name="ffn_moe_grouped",
    )(tile_expert, n_used, xs, wg, wu, wd)


def _router_kernel(h_ref, w_ref, b_ref, idx_ref, wt_ref, *, n_experts):
    logits = jnp.dot(h_ref[...], w_ref[...], preferred_element_type=F32,
                     precision=lax.Precision.HIGHEST) + b_ref[...]
    lane = _lane_iota(logits.shape)
    neg = jnp.float32(-jnp.inf)
    logits = jnp.where(lane < n_experts, logits, neg)
    m1 = jnp.max(logits, axis=-1, keepdims=True)
    i1 = jnp.min(jnp.where(logits == m1, lane, LANES), axis=-1, keepdims=True)
    rest = jnp.where(lane == i1, neg, logits)
    m2 = jnp.max(rest, axis=-1, keepdims=True)
    i2 = jnp.min(jnp.where(rest == m2, lane, LANES), axis=-1, keepdims=True)
    e2 = jnp.exp(m2 - m1)
    w1 = 1.0 / (1.0 + e2)
    w2 = e2 / (1.0 + e2)
    idx_ref[...] = jnp.where(lane == 0, i1, jnp.where(lane == 1, i2, 0))
    wt_ref[...] = jnp.where(lane == 0, w1, jnp.where(lane == 1, w2, 0.0))


def _router(h, w_pad, b_pad, n_experts):
    t, d = h.shape
    tm = TOKEN_TILE
    row = lambda n: pl.BlockSpec((tm, n), lambda i: (i, 0))
    full = lambda a: pl.BlockSpec(a.shape, lambda i: (0, 0))
    return pl.pallas_call(
        functools.partial(_router_kernel, n_experts=n_experts),
        grid=(t // tm,),
        in_specs=[row(d), full(w_pad), full(b_pad)],
        out_specs=[row(LANES), row(LANES)],
        out_shape=[jax.ShapeDtypeStruct((t, LANES), jnp.int32),
                   jax.ShapeDtypeStruct((t, LANES), F32)],
        compiler_params=_params("arbitrary"),
        name="moe_router",
    )(h, w_pad, b_pad)


def _stage_slots(slots_hbm, tile, slots_smem, sem):
    stage = pltpu.make_async_copy(slots_hbm.at[tile], slots_smem, sem)
    stage.start()
    stage.wait()


def _dispatch_copy(h_ref, xs_hbm, r, dst_row, sem):
    return pltpu.make_async_copy(h_ref.at[pl.ds(r, 1), :], xs_hbm.at[pl.ds(dst_row, 1), :], sem)


def _moe_dispatch_kernel(slots_hbm, h_ref, xs_in, xs_hbm, slots_smem, slot_sem, row_sem):
    del xs_in
    tm = h_ref.shape[0]
    _stage_slots(slots_hbm, pl.program_id(0), slots_smem, slot_sem)

    def issue(r, carry):
        _dispatch_copy(h_ref, xs_hbm, r, slots_smem[r], row_sem).start()
        _dispatch_copy(h_ref, xs_hbm, r, slots_smem[tm + r], row_sem).start()
        return carry

    lax.fori_loop(0, tm, issue, 0, unroll=8)

    def drain(r, carry):
        _dispatch_copy(h_ref, xs_hbm, r, 0, row_sem).wait()
        _dispatch_copy(h_ref, xs_hbm, r, 0, row_sem).wait()
        return carry

    lax.fori_loop(0, tm, drain, 0, unroll=8)


def _moe_dispatch(h, slots, n_rows):
    t, d = h.shape
    nt, two_tm = slots.shape
    tm = two_tm // TOP_K
    anyspec = pl.BlockSpec(memory_space=pl.ANY)
    return pl.pallas_call(
        _moe_dispatch_kernel,
        grid=(nt,),
        in_specs=[anyspec, pl.BlockSpec((tm, d), lambda i: (i, 0)), anyspec],
        out_specs=anyspec,
        out_shape=jax.ShapeDtypeStruct((n_rows, d), F32),
        scratch_shapes=[pltpu.SMEM((two_tm,), jnp.int32), pltpu.SemaphoreType.DMA,
                        pltpu.SemaphoreType.DMA],
        input_output_aliases={2: 0},
        compiler_params=_params("arbitrary"),
        name="moe_dispatch",
    )(slots, h, jnp.zeros((n_rows, d), F32))


def _combine_copy(ys_hbm, src_row, buf, slot, r, sems):
    return pltpu.make_async_copy(ys_hbm.at[pl.ds(src_row, 1), :],
                                 buf.at[slot, pl.ds(r, 1), :], sems.at[slot])


def _moe_combine_kernel(slots_hbm, ys_hbm, wt_ref, res_ref, g_ref, b_ref, of_ref, ob_ref,
                        buf, slots_smem, slot_sem, sems):
    i = pl.program_id(0)
    n = pl.num_programs(0)
    tm = res_ref.shape[0]
    n_rows = TOP_K * tm

    def issue_tile(tile, slot):
        _stage_slots(slots_hbm, tile, slots_smem, slot_sem)

        def issue(r, carry):
            _combine_copy(ys_hbm, slots_smem[r], buf, slot, r, sems).start()
            return carry

        lax.fori_loop(0, n_rows, issue, 0, unroll=8)

    @pl.when(i == 0)
    def _():
        issue_tile(0, 0)

    @pl.when(i + 1 < n)
    def _():
        issue_tile(i + 1, (i + 1) % 2)

    slot = i % 2

    def drain(r, carry):
        _combine_copy(ys_hbm, 0, buf, slot, r, sems).wait()
        return carry

    lax.fori_loop(0, n_rows, drain, 0, unroll=8)

    w0 = wt_ref[:, 0:1]
    w1 = wt_ref[:, 1:2]
    ffn = w0 * buf[slot, 0:tm, :] + w1 * buf[slot, tm:n_rows, :]
    out = _layer_norm(DN_ALPHA * res_ref[...] + ffn, g_ref[...], b_ref[...])
    of_ref[...] = out
    ob_ref[...] = out.astype(BF16)


def _moe_combine(ys, slots, wts, res, g, b):
    t, d = res.shape
    nt, two_tm = slots.shape
    tm = two_tm // TOP_K
    row = lambda n: pl.BlockSpec((tm, n), lambda i: (i, 0))
    vec = pl.BlockSpec((1, d), lambda i: (0, 0))
    anyspec = pl.BlockSpec(memory_space=pl.ANY)
    return pl.pallas_call(
        _moe_combine_kernel,
        grid=(nt,),
        in_specs=[anyspec, anyspec, row(LANES), row(d), vec, vec],
        out_specs=[row(d), row(d)],
        out_shape=[jax.ShapeDtypeStruct((t, d), F32), jax.ShapeDtypeStruct((t, d), BF16)],
        scratch_shapes=[pltpu.VMEM((2, two_tm, d), F32), pltpu.SMEM((two_tm,), jnp.int32),
                        pltpu.SemaphoreType.DMA, pltpu.SemaphoreType.DMA((2,))],
        compiler_params=_params("arbitrary"),
        name="moe_combine_ln",
    )(slots, ys, wts, res, g, b)


def _moe_slot_tables(idx, tm, tok):
    t = idx.shape[0]
    n_slots = TOP_K * t
    n_rows = n_slots + N_EXPERTS * tm
    nt = n_rows // tm
    ef = idx.reshape(-1)
    onehot = (ef[:, None] == jnp.arange(N_EXPERTS, dtype=jnp.int32)[None, :]).astype(jnp.int32)
    csum = jnp.cumsum(onehot, axis=0)
    rank = jnp.sum(csum * onehot, axis=1) - 1
    counts = csum[-1]
    padded = ((counts + tm - 1) // tm) * tm
    ends = jnp.cumsum(padded)
    starts = ends - padded
    dest = jnp.sum(starts[None, :] * onehot, axis=1) + rank
    n_used = (ends[-1] // tm).astype(jnp.int32).reshape(1)
    tile_start = jnp.arange(nt, dtype=jnp.int32) * tm
    tile_expert = jnp.sum((tile_start[:, None] >= ends[None, :]).astype(jnp.int32), axis=1)
    tile_expert = jnp.minimum(tile_expert, N_EXPERTS - 1).astype(jnp.int32)
    slots = dest.reshape(t // tok, tok, TOP_K).transpose(0, 2, 1).reshape(t // tok, TOP_K * tok)
    return slots.astype(jnp.int32), tile_expert, n_used, n_rows


def _moe_layer(h, w_router, b_router, wg, wu, wd, g, b):
    t, d = h.shape
    tm = TOKEN_TILE
    w_pad = jnp.zeros((d, LANES), F32).at[:, :N_EXPERTS].set(w_router)
    b_pad = jnp.zeros((1, LANES), F32).at[0, :N_EXPERTS].set(b_router)
    idx, wts = _router(h, w_pad, b_pad, N_EXPERTS)
    slots, tile_expert, n_used, n_rows = _moe_slot_tables(
        idx[:, :TOP_K], tm, min(MOE_TOKEN_TILE, t))
    xs = _moe_dispatch(h, slots, n_rows)
    ys = _ffn_moe(xs, wg, wu, wd, tile_expert, n_used)
    return _moe_combine(ys, slots, wts, h, g, b)


def _rope_tables(pos_ref, invf_ref):
    ang = pos_ref[...] * invf_ref[...]
    return jnp.cos(ang), jnp.sin(ang)


def _rope_block(v, cos, sin):
    half = MLA_ROPE // 2
    lane = _lane_iota(v.shape)
    up = pltpu.roll(v, LANES - half, 1)
    down = pltpu.roll(v, half, 1)
    rot = jnp.where(lane < half, -up, down)
    return jnp.where(lane < MLA_ROPE, v * cos + rot * sin, 0.0)


def _rms(v, g):
    return v * lax.rsqrt(jnp.mean(v * v, axis=-1, keepdims=True) + RMS_EPS) * g


def _mla_kv_kernel(h_ref, pos_ref, invf_ref, wc_ref, wr_ref, g_ref, wkn_ref, wv_ref,
                   k_ref, v_ref):
    hb = h_ref[...]
    ckv = _rms(_dot(hb, wc_ref[...]), g_ref[...]).astype(BF16)
    cos, sin = _rope_tables(pos_ref, invf_ref)
    kr = _rope_block(_dot(hb, wr_ref[...]), cos, sin).astype(BF16)
    kn = _dot(ckv, wkn_ref[...]).astype(BF16)
    vv = _dot(ckv, wv_ref[...]).astype(BF16)
    ones = jnp.ones((hb.shape[0], LANES), BF16)
    for hd in range(MLA_N_HEADS):
        lo, mid, hi = hd * 2 * LANES, hd * 2 * LANES + LANES, (hd + 1) * 2 * LANES
        k_ref[:, lo:mid] = kn[:, hd * LANES:(hd + 1) * LANES]
        k_ref[:, mid:hi] = kr
        v_ref[:, lo:mid] = vv[:, hd * LANES:(hd + 1) * LANES]
        v_ref[:, mid:hi] = ones


def _mla_q_kernel(h_ref, pos_ref, invf_ref, wa_ref, g_ref, wqn_ref, wqr_ref, q_ref):
    cq = _rms(_dot(h_ref[...], wa_ref[...]), g_ref[...]).astype(BF16)
    cos, sin = _rope_tables(pos_ref, invf_ref)
    scale = MLA_SCALE * LOG2_E
    qn = _dot(cq, wqn_ref[...]) * scale
    qr = _dot(cq, wqr_ref[...]) * scale
    for hd in range(MLA_N_HEADS):
        lo, mid, hi = hd * 2 * LANES, hd * 2 * LANES + LANES, (hd + 1) * 2 * LANES
        q_ref[:, lo:mid] = qn[:, hd * LANES:(hd + 1) * LANES].astype(BF16)
        q_ref[:, mid:hi] = _rope_block(qr[:, hd * LANES:(hd + 1) * LANES], cos, sin).astype(BF16)


def _token_call(kernel, name, t, operands, out_widths):
    tm = TOKEN_TILE
    row = lambda n: pl.BlockSpec((tm, n), lambda i: (i, 0))
    full = lambda a: pl.BlockSpec(a.shape, lambda i: (0, 0))
    return pl.pallas_call(
        kernel,
        grid=(t // tm,),
        in_specs=[row(a.shape[1]) if kind == "row" else full(a) for kind, a in operands],
        out_specs=[row(n) for n in out_widths],
        out_shape=[jax.ShapeDtypeStruct((t, n), BF16) for n in out_widths],
        compiler_params=_params("arbitrary"),
        name=name,
    )(*[a for _, a in operands])


def _flash_kernel(qi_ref, kj_ref, q_ref, k_ref, v_ref, o_ref, m_ref, acc_ref, *, chain):
    s_idx = pl.program_id(2)
    qi = qi_ref[s_idx]
    kj = kj_ref[s_idx]
    tq = q_ref.shape[0]
    n_chain = tq // chain

    @pl.when(kj == 0)
    def _():
        m_ref[...] = jnp.full(m_ref.shape, -jnp.inf, F32)
        acc_ref[...] = jnp.zeros(acc_ref.shape, F32)

    def scores(rows, lo, hi):
        return lax.dot_general(q_ref[rows, :], k_ref[lo:hi, :], (((1,), (1,)), ((), ())),
                               preferred_element_type=F32)

    def chain_update(r, n_open, masked):
        rows = slice(r * chain, (r + 1) * chain)
        m_new = m_old = m_ref[rows, :]
        parts = []
        if n_open:
            parts.append((scores(rows, 0, n_open), 0, n_open))
        if masked:
            ri = lax.broadcasted_iota(jnp.int32, (chain, chain), 0)
            ci = lax.broadcasted_iota(jnp.int32, (chain, chain), 1)
            sc = jnp.where(ci <= ri, scores(rows, n_open, n_open + chain), -jnp.inf)
            parts.append((sc, n_open, n_open + chain))
        for sc, _, _ in parts:
            m_new = jnp.maximum(m_new, jnp.max(sc, axis=-1, keepdims=True))
        acc = jnp.exp2(m_old - m_new) * acc_ref[rows, :]
        for sc, lo, hi in parts:
            acc = acc + _dot(jnp.exp2(sc - m_new).astype(BF16), v_ref[lo:hi, :])
        acc_ref[rows, :] = acc
        m_ref[rows, :] = m_new

    @pl.when(kj < qi)
    def _():
        for r in range(n_chain):
            chain_update(r, k_ref.shape[0], False)

    @pl.when(kj == qi)
    def _():
        for r in reversed(range(n_chain)):
            chain_update(r, r * chain, True)
        o_ref[...] = (acc_ref[:, 0:MLA_V] / acc_ref[:, MLA_V:2 * MLA_V]).astype(o_ref.dtype)


def _flash_attention(q, k, v, *, batch, seq):
    t = q.shape[0]
    ta = min(ATTN_TILE, seq)
    chain = min(ATTN_CHAIN, ta)
    nq = seq // ta
    pairs = [(i, j) for i in range(nq) for j in range(i + 1)]
    qi = jnp.asarray(np.array([p[0] for p in pairs], np.int32))
    kj = jnp.asarray(np.array([p[1] for p in pairs], np.int32))
    kw = 2 * LANES
    q_spec = pl.BlockSpec((ta, kw), lambda b, h, s, qi, kj: (b * nq + qi[s], h))
    kv_spec = pl.BlockSpec((ta, kw), lambda b, h, s, qi, kj: (b * nq + kj[s], h))
    o_spec = pl.BlockSpec((ta, MLA_V), lambda b, h, s, qi, kj: (b * nq + qi[s], h))
    return pl.pallas_call(
        functools.partial(_flash_kernel, chain=chain),
        grid_spec=pltpu.PrefetchScalarGridSpec(
            num_scalar_prefetch=2, grid=(batch, MLA_N_HEADS, len(pairs)),
            in_specs=[q_spec, kv_spec, kv_spec],
            out_specs=o_spec,
            scratch_shapes=[pltpu.VMEM((ta, 1), F32), pltpu.VMEM((ta, kw), F32)]),
        out_shape=jax.ShapeDtypeStruct((t, MLA_N_HEADS * MLA_V), BF16),
        compiler_params=_params("arbitrary", "arbitrary", "arbitrary"),
        name="mla_flash_attention",
    )(qi, kj, q, k, v)


def _pad_lanes(w):
    k, nh, r = w.shape
    return jnp.zeros((k, nh, LANES), w.dtype).at[:, :, :r].set(w).reshape(k, nh * LANES)


def kernel(x, positions, ln_g, ln_b, ssm_w_in, ssm_conv_w, ssm_conv_b, ssm_dt_bias, ssm_a_log, ssm_d, ssm_norm_g, ssm_w_out, mla_w_kv_a, mla_kv_norm_g, mla_w_kv_b, mla_w_q_a, mla_q_norm_g, mla_w_q_b, mla_w_o, ffn_w_gate, ffn_w_up, ffn_w_down, moe_w_router, moe_b_router, moe_w_gate, moe_w_up, moe_w_down):
    batch, seq, d = x.shape
    t = batch * seq
    n_a = ssm_w_in.shape[0]
    n_heads = ssm_a_log.shape[1]
    d_inner = n_heads * SSM_HEAD_DIM
    conv_dim = ssm_conv_w.shape[2]
    ta = min(ATTN_TILE, seq)
    assert seq % SSM_CHUNK == 0 and t % TOKEN_TILE == 0 and t % min(MOE_TOKEN_TILE, t) == 0
    assert seq % ta == 0 and ta % min(ATTN_CHAIN, ta) == 0
    assert n_heads <= LANES and n_heads % (2 * SSM_N_GROUPS) == 0

    h = x.reshape(t, d)
    hb = h.astype(BF16)
    pos = positions.reshape(t, 1).astype(F32)
    inv_freq = ROPE_BASE ** (-jnp.arange(0, MLA_ROPE, 2, dtype=F32) / MLA_ROPE)
    invf = jnp.tile(inv_freq, LANES // (MLA_ROPE // 2)).reshape(1, LANES)
    vec = lambda a: a.reshape(1, -1)
    pad_heads = lambda a: jnp.zeros((1, LANES), F32).at[0, :n_heads].set(a)

    k_cat = v_cat = None
    for layer in range(DEPTH):
        g0, b0 = vec(ln_g[layer, 0]), vec(ln_b[layer, 0])
        g1, b1 = vec(ln_g[layer, 1]), vec(ln_b[layer, 1])
        if layer < n_a:
            a = layer
            w_in = ssm_w_in[a].astype(BF16)
            wz = w_in[:, :d_inner]
            wx = w_in[:, d_inner:d_inner + conv_dim]
            wdt = jnp.zeros((d, LANES), BF16).at[:, :n_heads].set(w_in[:, d_inner + conv_dim:])
            z, xbc, dt = _inproj(hb, wz, wx, wdt)
            y = _ssd(z, xbc, dt, ssm_conv_w[a], vec(ssm_conv_b[a]), pad_heads(ssm_dt_bias[a]),
                     pad_heads(ssm_a_log[a]), vec(jnp.repeat(ssm_d[a], SSM_HEAD_DIM)),
                     vec(ssm_norm_g[a]), batch=batch, seq=seq, n_heads=n_heads)
            h, hb = _proj_ln(y, ssm_w_out[a].astype(BF16), h, g0, b0)
        else:
            bi = layer - n_a
            cat_width = MLA_N_HEADS * 2 * LANES
            if bi == 0:
                kv_lora = mla_kv_norm_g.shape[0]
                wc = mla_w_kv_a[:, :kv_lora].astype(BF16)
                wr = _pad_lanes(mla_w_kv_a[:, kv_lora:].reshape(d, 1, MLA_ROPE)).astype(BF16)
                wkv = mla_w_kv_b.reshape(kv_lora, MLA_N_HEADS, MLA_NOPE + MLA_V)
                wkn = wkv[:, :, :MLA_NOPE].reshape(kv_lora, -1).astype(BF16)
                wv = wkv[:, :, MLA_NOPE:].reshape(kv_lora, -1).astype(BF16)
                k_cat, v_cat = _token_call(
                    _mla_kv_kernel, "mla_shared_kv", t,
                    [("row", hb), ("row", pos), ("full", invf), ("full", wc), ("full", wr),
                     ("full", vec(mla_kv_norm_g)), ("full", wkn), ("full", wv)],
                    [cat_width, cat_width])
            q_lora = mla_q_norm_g.shape[1]
            wqb = mla_w_q_b[bi].reshape(q_lora, MLA_N_HEADS, MLA_QK)
            wqn = wqb[:, :, :MLA_NOPE].reshape(q_lora, -1).astype(BF16)
            wqr = _pad_lanes(wqb[:, :, MLA_NOPE:]).astype(BF16)
            (q_cat,) = _token_call(
                _mla_q_kernel, "mla_q_proj", t,
                [("row", hb), ("row", pos), ("full", invf), ("full", mla_w_q_a[bi].astype(BF16)),
                 ("full", vec(mla_q_norm_g[bi])), ("full", wqn), ("full", wqr)],
                [cat_width])
            o = _flash_attention(q_cat, k_cat, v_cat, batch=batch, seq=seq)
            h, hb = _proj_ln(o, mla_w_o[bi].astype(BF16), h, g0, b0)
        ci = layer // 2
        if layer % 2 == 0:
            h, hb = _ffn_dense(hb, ffn_w_gate[ci].astype(BF16), ffn_w_up[ci].astype(BF16),
                               ffn_w_down[ci].astype(BF16), h, g1, b1)
        else:
            h, hb = _moe_layer(h, moe_w_router[ci], moe_b_router[ci],
                               moe_w_gate[ci].astype(BF16), moe_w_up[ci].astype(BF16),
                               moe_w_down[ci].astype(BF16), g1, b1)
    return h.reshape(batch, seq, d)
```

```python
import functools
import math

import numpy as np
import jax
import jax.numpy as jnp
from jax import lax
from jax.experimental import pallas as pl
from jax.experimental.pallas import tpu as pltpu

F32 = jnp.float32
BF16 = jnp.bfloat16

DEPTH = 4
DN_ALPHA = (2.0 * DEPTH) ** 0.25
LN_EPS = 1e-5
RMS_EPS = 1e-6
SSM_HEAD_DIM = 64
SSM_N_GROUPS = 4
SSM_D_STATE = 128
SSM_CHUNK = 128
SSM_CONV = 4
MLA_N_HEADS = 8
MLA_NOPE = 128
MLA_ROPE = 64
MLA_V = 128
MLA_QK = MLA_NOPE + MLA_ROPE
MLA_SCALE = MLA_QK ** -0.5
ROPE_BASE = 10000.0
N_EXPERTS = 8
TOP_K = 2
LOG2_E = math.log2(math.e)

LANES = 128
SUBLANES = 8
VMEM_LIMIT = 56 * 1024 * 1024

TOKEN_TILE = 512
FF_CHUNK = 256
MOE_TOKEN_TILE = 1024
ATTN_TILE = 2048
ATTN_CHAIN = 256


def _params(*sem):
    return pltpu.CompilerParams(dimension_semantics=sem, vmem_limit_bytes=VMEM_LIMIT)


def _dot(a, b):
    return jnp.dot(a, b, preferred_element_type=F32)


def _silu(v):
    return v * (1.0 / (1.0 + jnp.exp(-v)))


def _layer_norm(v, g, b):
    mu = jnp.mean(v, axis=-1, keepdims=True)
    vc = v - mu
    var = jnp.mean(vc * vc, axis=-1, keepdims=True)
    return vc * lax.rsqrt(var + LN_EPS) * g + b


def _lane_iota(shape):
    return lax.broadcasted_iota(jnp.int32, shape, len(shape) - 1)


def _inproj_kernel(x_ref, wz_ref, wx_ref, wdt_ref, z_ref, xbc_ref, dt_ref):
    x = x_ref[...]
    z_ref[...] = _dot(x, wz_ref[...]).astype(z_ref.dtype)
    xbc_ref[...] = _dot(x, wx_ref[...]).astype(xbc_ref.dtype)
    dt_ref[...] = _dot(x, wdt_ref[...])


def _inproj(xb, wz, wx, wdt):
    t, d = xb.shape
    tm = TOKEN_TILE
    full = lambda w: pl.BlockSpec(w.shape, lambda i: (0, 0))
    row = lambda n: pl.BlockSpec((tm, n), lambda i: (i, 0))
    return pl.pallas_call(
        _inproj_kernel,
        grid=(t // tm,),
        in_specs=[row(d), full(wz), full(wx), full(wdt)],
        out_specs=[row(wz.shape[1]), row(wx.shape[1]), row(wdt.shape[1])],
        out_shape=[jax.ShapeDtypeStruct((t, wz.shape[1]), BF16),
                   jax.ShapeDtypeStruct((t, wx.shape[1]), BF16),
                   jax.ShapeDtypeStruct((t, wdt.shape[1]), F32)],
        compiler_params=_params("arbitrary"),
        name="ssm_in_proj",
    )(xb, wz, wx, wdt)


def _ssd_kernel(z_ref, xbc_ref, dt_ref, cw_ref, cb_ref, dtb_ref, alog_ref, dsk_ref, ng_ref,
                y_ref, ubuf, xc, ybuf, state, *, n_heads, d_inner, gn):
    L = SSM_CHUNK
    P = SSM_HEAD_DIM
    n_state = SSM_D_STATE
    conv_dim = d_inner + 2 * gn
    halo = SUBLANES
    c = pl.program_id(1)

    @pl.when(c == 0)
    def _():
        ubuf[0:halo, :] = jnp.zeros((halo, conv_dim), F32)
        state[...] = jnp.zeros(state.shape, F32)

    ubuf[halo:halo + L, :] = xbc_ref[...].astype(F32)
    cblk = 512
    for j in range(conv_dim // cblk):
        cs = slice(j * cblk, (j + 1) * cblk)
        acc = cb_ref[:, cs] + cw_ref[3:4, cs] * ubuf[halo:halo + L, cs]
        for k in range(SSM_CONV - 1):
            sh = SSM_CONV - 1 - k
            acc = acc + cw_ref[k:k + 1, cs] * ubuf[halo - sh:halo - sh + L, cs]
        xc[:, cs] = _silu(acc)
    ubuf[0:halo, :] = ubuf[L:L + halo, :]

    lane = _lane_iota((L, LANES))
    head_ok = lane < n_heads
    dtr = dt_ref[...] + dtb_ref[...]
    dtp = jnp.maximum(dtr, 0.0) + jnp.log1p(jnp.exp(-jnp.abs(dtr)))
    dtp = jnp.where(head_ok, dtp, 0.0)
    a_row = jnp.where(head_ok[0:1, :], -jnp.exp(alog_ref[...]), 0.0)
    la = dtp * a_row
    row_i = lax.broadcasted_iota(jnp.int32, (L, L), 0)
    col_i = lax.broadcasted_iota(jnp.int32, (L, L), 1)
    causal = row_i >= col_i
    tril = jnp.where(causal, 1.0, 0.0).astype(F32)
    cum = jnp.dot(tril, la, preferred_element_type=F32, precision=lax.Precision.HIGHEST)
    cum_t = cum.T
    dt_t = dtp.T
    ecum = jnp.exp(cum)
    last_t = cum_t[:, L - 1:L]
    w_t = jnp.exp(last_t - cum_t) * dt_t
    edec_t = jnp.exp(last_t)
    lane_lo = _lane_iota((L, 2 * P)) < P

    b_off = d_inner
    c_off = d_inner + gn
    heads_per_group = n_heads // SSM_N_GROUPS
    for g in range(SSM_N_GROUPS):
        bg = xc[:, b_off + g * n_state: b_off + (g + 1) * n_state]
        cg = xc[:, c_off + g * n_state: c_off + (g + 1) * n_state]
        bg_t = bg.T
        cb = _dot(cg.astype(BF16), bg_t.astype(BF16))
        for pr in range(heads_per_group // 2):
            h0 = g * heads_per_group + 2 * pr
            pair = h0 // 2
            xs_pair = xc[:, h0 * P:(h0 + 2) * P].astype(BF16)
            st_pair = state[pair]
            rhs = jnp.concatenate([xs_pair, st_pair.astype(BF16)], axis=0)
            ys, sts, decs = [], [], []
            for h in (h0, h0 + 1):
                seg = cum[:, h:h + 1] - cum_t[h:h + 1, :]
                dec = jnp.where(causal, jnp.exp(seg), 0.0)
                gmat = cb * dec * dt_t[h:h + 1, :]
                cs_h = cg * ecum[:, h:h + 1]
                lhs = jnp.concatenate([gmat.astype(BF16), cs_h.astype(BF16)], axis=1)
                ys.append(_dot(lhs, rhs))
                bw_t = (bg_t * w_t[h:h + 1, :]).astype(BF16)
                sts.append(_dot(bw_t, xs_pair))
                decs.append(edec_t[h:h + 1, :])
            ybuf[:, h0 * P:(h0 + 2) * P] = jnp.where(lane_lo, ys[0], ys[1])
            dec_pair = jnp.where(lane_lo, decs[0], decs[1])
            state[pair] = dec_pair * st_pair + jnp.where(lane_lo, sts[0], sts[1])

    gw = d_inner // SSM_N_GROUPS
    for g in range(SSM_N_GROUPS):
        cs = slice(g * gw, (g + 1) * gw)
        y = ybuf[:, cs] + dsk_ref[:, cs] * xc[:, cs]
        y = y * _silu(z_ref[:, cs].astype(F32))
        ms = jnp.mean(y * y, axis=-1, keepdims=True)
        y_ref[:, cs] = (y * lax.rsqrt(ms + LN_EPS) * ng_ref[:, cs]).astype(y_ref.dtype)


def _ssd(z, xbc, dt, conv_w, conv_b, dt_bias, a_log, d_skip, norm_g, *, batch, seq, n_heads):
    t, d_inner = z.shape
    conv_dim = xbc.shape[1]
    gn = (conv_dim - d_inner) // 2
    L = SSM_CHUNK
    nc = seq // L
    row = lambda n: pl.BlockSpec((L, n), lambda b, c: (b * nc + c, 0))
    full = lambda a: pl.BlockSpec(a.shape, lambda b, c: (0, 0))
    kern = functools.partial(_ssd_kernel, n_heads=n_heads, d_inner=d_inner, gn=gn)
    return pl.pallas_call(
        kern,
        grid=(batch, nc),
        in_specs=[row(d_inner), row(conv_dim), row(LANES), full(conv_w), full(conv_b),
                  full(dt_bias), full(a_log), full(d_skip), full(norm_g)],
        out_specs=row(d_inner),
        out_shape=jax.ShapeDtypeStruct((t, d_inner), BF16),
        scratch_shapes=[pltpu.VMEM((L + 2 * SUBLANES, conv_dim), F32),
                        pltpu.VMEM((L, conv_dim), F32),
                        pltpu.VMEM((L, d_inner), F32),
                        pltpu.VMEM((n_heads // 2, SSM_D_STATE, 2 * SSM_HEAD_DIM), F32)],
        compiler_params=_params("arbitrary", "arbitrary"),
        name="ssm_conv_ssd",
    )(z, xbc, dt, conv_w, conv_b, dt_bias, a_log, d_skip, norm_g)


def _proj_ln_kernel(x_ref, w_ref, res_ref, g_ref, b_ref, of_ref, ob_ref):
    mix = _dot(x_ref[...], w_ref[...])
    out = _layer_norm(DN_ALPHA * res_ref[...] + mix, g_ref[...], b_ref[...])
    of_ref[...] = out
    ob_ref[...] = out.astype(BF16)


def _proj_ln(xb, w, res, g, b):
    t, k = xb.shape
    d = w.shape[1]
    tm = TOKEN_TILE
    row = lambda n: pl.BlockSpec((tm, n), lambda i: (i, 0))
    full = lambda a: pl.BlockSpec(a.shape, lambda i: (0, 0))
    return pl.pallas_call(
        _proj_ln_kernel,
        grid=(t // tm,),
        in_specs=[row(k), full(w), row(d), full(g), full(b)],
        out_specs=[row(d), row(d)],
        out_shape=[jax.ShapeDtypeStruct((t, d), F32), jax.ShapeDtypeStruct((t, d), BF16)],
        compiler_params=_params("arbitrary"),
        name="proj_residual_ln",
    )(xb, w, res, g, b)


def _ffn_tile(x, wg_ref, wu_ref, wd_ref):
    dff = wg_ref.shape[2]
    acc = None
    for c in range(dff // FF_CHUNK):
        cs = slice(c * FF_CHUNK, (c + 1) * FF_CHUNK)
        gate = _dot(x, wg_ref[0, :, cs])
        up = _dot(x, wu_ref[0, :, cs])
        hid = (_silu(gate) * up).astype(BF16)
        part = _dot(hid, wd_ref[0, cs, :])
        acc = part if acc is None else acc + part
    return acc


def _ffn_dense_kernel(te_ref, nt_ref, x_ref, wg_ref, wu_ref, wd_ref, res_ref, g_ref, b_ref,
                      of_ref, ob_ref):
    ffn = _ffn_tile(x_ref[...], wg_ref, wu_ref, wd_ref)
    out = _layer_norm(DN_ALPHA * res_ref[...] + ffn, g_ref[...], b_ref[...])
    of_ref[...] = out
    ob_ref[...] = out.astype(BF16)


def _ffn_moe_kernel(te_ref, nt_ref, x_ref, wg_ref, wu_ref, wd_ref, o_ref):
    used = pl.program_id(0) < nt_ref[0]

    @pl.when(used)
    def _():
        o_ref[...] = _ffn_tile(x_ref[...].astype(BF16), wg_ref, wu_ref, wd_ref)

    @pl.when(jnp.logical_not(used))
    def _():
        o_ref[...] = jnp.zeros(o_ref.shape, F32)


def _ffn_specs(d, dff, tm):
    assert dff % FF_CHUNK == 0
    x_spec = pl.BlockSpec((tm, d), lambda i, te, nt: (i, 0))
    wgu_spec = pl.BlockSpec((1, d, dff), lambda i, te, nt: (te[i], 0, 0))
    wd_spec = pl.BlockSpec((1, dff, d), lambda i, te, nt: (te[i], 0, 0))
    return x_spec, wgu_spec, wd_spec


def _ffn_dense(xb, wg, wu, wd, res, g, b):
    t, d = xb.shape
    tm = TOKEN_TILE
    nt = t // tm
    x_spec, wgu_spec, wd_spec = _ffn_specs(d, wg.shape[-1], tm)
    vec = pl.BlockSpec((1, d), lambda i, te, n: (0, 0))
    te = jnp.zeros((nt,), jnp.int32)
    n_used = jnp.full((1,), nt, jnp.int32)
    return pl.pallas_call(
        _ffn_dense_kernel,
        grid_spec=pltpu.PrefetchScalarGridSpec(
            num_scalar_prefetch=2, grid=(nt,),
            in_specs=[x_spec, wgu_spec, wgu_spec, wd_spec, x_spec, vec, vec],
            out_specs=[x_spec, x_spec]),
        out_shape=[jax.ShapeDtypeStruct((t, d), F32), jax.ShapeDtypeStruct((t, d), BF16)],
        compiler_params=_params("arbitrary"),
        name="ffn_dense_ln",
    )(te, n_used, xb, wg[None], wu[None], wd[None], res, g, b)


def _ffn_moe(xs, wg, wu, wd, tile_expert, n_used):
    p, d = xs.shape
    tm = TOKEN_TILE
    nt = p // tm
    x_spec, wgu_spec, wd_spec = _ffn_specs(d, wg.shape[-1], tm)
    return pl.pallas_call(
        _ffn_moe_kernel,
        grid_spec=pltpu.PrefetchScalarGridSpec(
            num_scalar_prefetch=2, grid=(nt,),
            in_specs=[x_spec, wgu_spec, wgu_spec, wd_spec],
            out_specs=x_spec),
        out_shape=jax.ShapeDtypeStruct((p, d), F32),
        compiler_params=_params("arbitrary"),
        name="ffn_moe_grouped",
    )(tile_expert, n_used, xs, wg, wu, wd)


def _router_kernel(h_ref, w_ref, b_ref, idx_ref, wt_ref, *, n_experts):
    logits = jnp.dot(h_ref[...], w_ref[...], preferred_element_type=F32,
                     precision=lax.Precision.HIGHEST) + b_ref[...]
    lane = _lane_iota(logits.shape)
    neg = jnp.float32(-jnp.inf)
    logits = jnp.where(lane < n_experts, logits, neg)
    m1 = jnp.max(logits, axis=-1, keepdims=True)
    i1 = jnp.min(jnp.where(logits == m1, lane, LANES), axis=-1, keepdims=True)
    rest = jnp.where(lane == i1, neg, logits)
    m2 = jnp.max(rest, axis=-1, keepdims=True)
    i2 = jnp.min(jnp.where(rest == m2, lane, LANES), axis=-1, keepdims=True)
    e2 = jnp.exp(m2 - m1)
    w1 = 1.0 / (1.0 + e2)
    w2 = e2 / (1.0 + e2)
    idx_ref[...] = jnp.where(lane == 0, i1, jnp.where(lane == 1, i2, 0))
    wt_ref[...] = jnp.where(lane == 0, w1, jnp.where(lane == 1, w2, 0.0))


def _router(h, w_pad, b_pad, n_experts):
    t, d = h.shape
    tm = TOKEN_TILE
    row = lambda n: pl.BlockSpec((tm, n), lambda i: (i, 0))
    full = lambda a: pl.BlockSpec(a.shape, lambda i: (0, 0))
    return pl.pallas_call(
        functools.partial(_router_kernel, n_experts=n_experts),
        grid=(t // tm,),
        in_specs=[row(d), full(w_pad), full(b_pad)],
        out_specs=[row(LANES), row(LANES)],
        out_shape=[jax.ShapeDtypeStruct((t, LANES), jnp.int32),
                   jax.ShapeDtypeStruct((t, LANES), F32)],
        compiler_params=_params("arbitrary"),
        name="moe_router",
    )(h, w_pad, b_pad)


def _stage_slots(slots_hbm, tile, slots_smem, sem):
    stage = pltpu.make_async_copy(slots_hbm.at[tile], slots_smem, sem)
    stage.start()
    stage.wait()


def _dispatch_copy(h_ref, xs_hbm, r, dst_row, sem):
    return pltpu.make_async_copy(h_ref.at[pl.ds(r, 1), :], xs_hbm.at[pl.ds(dst_row, 1), :], sem)


def _moe_dispatch_kernel(slots_hbm, h_ref, xs_in, xs_hbm, slots_smem, slot_sem, row_sem):
    del xs_in
    tm = h_ref.shape[0]
    _stage_slots(slots_hbm, pl.program_id(0), slots_smem, slot_sem)

    def issue(r, carry):
        _dispatch_copy(h_ref, xs_hbm, r, slots_smem[r], row_sem).start()
        _dispatch_copy(h_ref, xs_hbm, r, slots_smem[tm + r], row_sem).start()
        return carry

    lax.fori_loop(0, tm, issue, 0, unroll=8)

    def drain(r, carry):
        _dispatch_copy(h_ref, xs_hbm, r, 0, row_sem).wait()
        _dispatch_copy(h_ref, xs_hbm, r, 0, row_sem).wait()
        return carry

    lax.fori_loop(0, tm, drain, 0, unroll=8)


def _moe_dispatch(h, slots, n_rows):
    t, d = h.shape
    nt, two_tm = slots.shape
    tm = two_tm // TOP_K
    anyspec = pl.BlockSpec(memory_space=pl.ANY)
    return pl.pallas_call(
        _moe_dispatch_kernel,
        grid=(nt,),
        in_specs=[anyspec, pl.BlockSpec((tm, d), lambda i: (i, 0)), anyspec],
        out_specs=anyspec,
        out_shape=jax.ShapeDtypeStruct((n_rows, d), F32),
        scratch_shapes=[pltpu.SMEM((two_tm,), jnp.int32), pltpu.SemaphoreType.DMA,
                        pltpu.SemaphoreType.DMA],
        input_output_aliases={2: 0},
        compiler_params=_params("arbitrary"),
        name="moe_dispatch",
    )(slots, h, jnp.zeros((n_rows, d), F32))


def _combine_copy(ys_hbm, src_row, buf, slot, r, sems):
    return pltpu.make_async_copy(ys_hbm.at[pl.ds(src_row, 1), :],
                                 buf.at[slot, pl.ds(r, 1), :], sems.at[slot])


def _moe_combine_kernel(slots_hbm, ys_hbm, wt_ref, res_ref, g_ref, b_ref, of_ref, ob_ref,
                        buf, slots_smem, slot_sem, sems):
    i = pl.program_id(0)
    n = pl.num_programs(0)
    tm = res_ref.shape[0]
    n_rows = TOP_K * tm

    def issue_tile(tile, slot):
        _stage_slots(slots_hbm, tile, slots_smem, slot_sem)

        def issue(r, carry):
            _combine_copy(ys_hbm, slots_smem[r], buf, slot, r, sems).start()
            return carry

        lax.fori_loop(0, n_rows, issue, 0, unroll=8)

    @pl.when(i == 0)
    def _():
        issue_tile(0, 0)

    @pl.when(i + 1 < n)
    def _():
        issue_tile(i + 1, (i + 1) % 2)

    slot = i % 2

    def drain(r, carry):
        _combine_copy(ys_hbm, 0, buf, slot, r, sems).wait()
        return carry

    lax.fori_loop(0, n_rows, drain, 0, unroll=8)

    w0 = wt_ref[:, 0:1]
    w1 = wt_ref[:, 1:2]
    ffn = w0 * buf[slot, 0:tm, :] + w1 * buf[slot, tm:n_rows, :]
    out = _layer_norm(DN_ALPHA * res_ref[...] + ffn, g_ref[...], b_ref[...])
    of_ref[...] = out
    ob_ref[...] = out.astype(BF16)


def _moe_combine(ys, slots, wts, res, g, b):
    t, d = res.shape
    nt, two_tm = slots.shape
    tm = two_tm // TOP_K
    row = lambda n: pl.BlockSpec((tm, n), lambda i: (i, 0))
    vec = pl.BlockSpec((1, d), lambda i: (0, 0))
    anyspec = pl.BlockSpec(memory_space=pl.ANY)
    return pl.pallas_call(
        _moe_combine_kernel,
        grid=(nt,),
        in_specs=[anyspec, anyspec, row(LANES), row(d), vec, vec],
        out_specs=[row(d), row(d)],
        out_shape=[jax.ShapeDtypeStruct((t, d), F32), jax.ShapeDtypeStruct((t, d), BF16)],
        scratch_shapes=[pltpu.VMEM((2, two_tm, d), F32), pltpu.SMEM((two_tm,), jnp.int32),
                        pltpu.SemaphoreType.DMA, pltpu.SemaphoreType.DMA((2,))],
        compiler_params=_params("arbitrary"),
        name="moe_combine_ln",
    )(slots, ys, wts, res, g, b)


def _moe_slot_tables(idx, tm, tok):
    t = idx.shape[0]
    n_slots = TOP_K * t
    n_rows = n_slots + N_EXPERTS * tm
    nt = n_rows // tm
    ef = idx.reshape(-1)
    onehot = (ef[:, None] == jnp.arange(N_EXPERTS, dtype=jnp.int32)[None, :]).astype(jnp.int32)
    csum = jnp.cumsum(onehot, axis=0)
    rank = jnp.sum(csum * onehot, axis=1) - 1
    counts = csum[-1]
    padded = ((counts + tm - 1) // tm) * tm
    ends = jnp.cumsum(padded)
    starts = ends - padded
    dest = jnp.sum(starts[None, :] * onehot, axis=1) + rank
    n_used = (ends[-1] // tm).astype(jnp.int32).reshape(1)
    tile_start = jnp.arange(nt, dtype=jnp.int32) * tm
    tile_expert = jnp.sum((tile_start[:, None] >= ends[None, :]).astype(jnp.int32), axis=1)
    tile_expert = jnp.minimum(tile_expert, N_EXPERTS - 1).astype(jnp.int32)
    slots = dest.reshape(t // tok, tok, TOP_K).transpose(0, 2, 1).reshape(t // tok, TOP_K * tok)
    return slots.astype(jnp.int32), tile_expert, n_used, n_rows


def _moe_layer(h, w_router, b_router, wg, wu, wd, g, b):
    t, d = h.shape
    tm = TOKEN_TILE
    w_pad = jnp.zeros((d, LANES), F32).at[:, :N_EXPERTS].set(w_router)
    b_pad = jnp.zeros((1, LANES), F32).at[0, :N_EXPERTS].set(b_router)
    idx, wts = _router(h, w_pad, b_pad, N_EXPERTS)
    slots, tile_expert, n_used, n_rows = _moe_slot_tables(
        idx[:, :TOP_K], tm, min(MOE_TOKEN_TILE, t))
    xs = _moe_dispatch(h, slots, n_rows)
    ys = _ffn_moe(xs, wg, wu, wd, tile_expert, n_used)
    return _moe_combine(ys, slots, wts, h, g, b)


def _rope_tables(pos_ref, invf_ref):
    ang = pos_ref[...] * invf_ref[...]
    return jnp.cos(ang), jnp.sin(ang)


def _rope_block(v, cos, sin):
    half = MLA_ROPE // 2
    lane = _lane_iota(v.shape)
    up = pltpu.roll(v, LANES - half, 1)
    down = pltpu.roll(v, half, 1)
    rot = jnp.where(lane < half, -up, down)
    return jnp.where(lane < MLA_ROPE, v * cos + rot * sin, 0.0)


def _rms(v, g):
    return v * lax.rsqrt(jnp.mean(v * v, axis=-1, keepdims=True) + RMS_EPS) * g


def _mla_kv_kernel(h_ref, pos_ref, invf_ref, wc_ref, wr_ref, g_ref, wkn_ref, wv_ref,
                   k_ref, v_ref, cos_ref, sin_ref):
    hb = h_ref[...]
    ckv = _rms(_dot(hb, wc_ref[...]), g_ref[...]).astype(BF16)
    cos, sin = _rope_tables(pos_ref, invf_ref)
    cos_ref[...] = cos
    sin_ref[...] = sin
    kr = _rope_block(_dot(hb, wr_ref[...]), cos, sin).astype(BF16)
    kn = _dot(ckv, wkn_ref[...]).astype(BF16)
    vv = _dot(ckv, wv_ref[...]).astype(BF16)
    ones = jnp.ones((hb.shape[0], LANES), BF16)
    for hd in range(MLA_N_HEADS):
        lo, mid, hi = hd * 2 * LANES, hd * 2 * LANES + LANES, (hd + 1) * 2 * LANES
        k_ref[:, lo:mid] = kn[:, hd * LANES:(hd + 1) * LANES]
        k_ref[:, mid:hi] = kr
        v_ref[:, lo:mid] = vv[:, hd * LANES:(hd + 1) * LANES]
        v_ref[:, mid:hi] = ones


def _mla_q_kernel(h_ref, cos_ref, sin_ref, wa_ref, g_ref, wqn_ref, wqr_ref, q_ref):
    cq = _rms(_dot(h_ref[...], wa_ref[...]), g_ref[...]).astype(BF16)
    cos, sin = cos_ref[...], sin_ref[...]
    scale = MLA_SCALE * LOG2_E
    qn = _dot(cq, wqn_ref[...]) * scale
    qr = _dot(cq, wqr_ref[...]) * scale
    for hd in range(MLA_N_HEADS):
        lo, mid, hi = hd * 2 * LANES, hd * 2 * LANES + LANES, (hd + 1) * 2 * LANES
        q_ref[:, lo:mid] = qn[:, hd * LANES:(hd + 1) * LANES].astype(BF16)
        q_ref[:, mid:hi] = _rope_block(qr[:, hd * LANES:(hd + 1) * LANES], cos, sin).astype(BF16)


def _token_call(kernel, name, t, operands, outs):
    tm = TOKEN_TILE
    row = lambda n: pl.BlockSpec((tm, n), lambda i: (i, 0))
    full = lambda a: pl.BlockSpec(a.shape, lambda i: (0, 0))
    return pl.pallas_call(
        kernel,
        grid=(t // tm,),
        in_specs=[row(a.shape[1]) if kind == "row" else full(a) for kind, a in operands],
        out_specs=[row(n) for n, _ in outs],
        out_shape=[jax.ShapeDtypeStruct((t, n), dt) for n, dt in outs],
        compiler_params=_params("arbitrary"),
        name=name,
    )(*[a for _, a in operands])


def _flash_kernel(qi_ref, kj_ref, q_ref, k_ref, v_ref, o_ref, m_ref, acc_ref, *, chain):
    s_idx = pl.program_id(2)
    qi = qi_ref[s_idx]
    kj = kj_ref[s_idx]
    tq = q_ref.shape[0]
    n_chain = tq // chain

    @pl.when(kj == 0)
    def _():
        m_ref[...] = jnp.full(m_ref.shape, -jnp.inf, F32)
        acc_ref[...] = jnp.zeros(acc_ref.shape, F32)

    def scores(rows, lo, hi):
        return lax.dot_general(q_ref[rows, :], k_ref[lo:hi, :], (((1,), (1,)), ((), ())),
                               preferred_element_type=F32)

    def chain_update(r, n_open, masked):
        rows = slice(r * chain, (r + 1) * chain)
        m_new = m_old = m_ref[rows, :]
        parts = []
        if n_open:
            parts.append((scores(rows, 0, n_open), 0, n_open))
        if masked:
            ri = lax.broadcasted_iota(jnp.int32, (chain, chain), 0)
            ci = lax.broadcasted_iota(jnp.int32, (chain, chain), 1)
            sc = jnp.where(ci <= ri, scores(rows, n_open, n_open + chain), -jnp.inf)
            parts.append((sc, n_open, n_open + chain))
        for sc, _, _ in parts:
            m_new = jnp.maximum(m_new, jnp.max(sc, axis=-1, keepdims=True))
        acc = jnp.exp2(m_old - m_new) * acc_ref[rows, :]
        for sc, lo, hi in parts:
            acc = acc + _dot(jnp.exp2(sc - m_new).astype(BF16), v_ref[lo:hi, :])
        acc_ref[rows, :] = acc
        m_ref[rows, :] = m_new

    @pl.when(kj < qi)
    def _():
        for r in range(n_chain):
            chain_update(r, k_ref.shape[0], False)

    @pl.when(kj == qi)
    def _():
        for r in reversed(range(n_chain)):
            chain_update(r, r * chain, True)
        o_ref[...] = (acc_ref[:, 0:MLA_V] / acc_ref[:, MLA_V:2 * MLA_V]).astype(o_ref.dtype)


def _flash_attention(q, k, v, *, batch, seq):
    t = q.shape[0]
    ta = min(ATTN_TILE, seq)
    chain = min(ATTN_CHAIN, ta)
    nq = seq // ta
    pairs = [(i, j) for i in range(nq) for j in range(i + 1)]
    qi = jnp.asarray(np.array([p[0] for p in pairs], np.int32))
    kj = jnp.asarray(np.array([p[1] for p in pairs], np.int32))
    kw = 2 * LANES
    q_spec = pl.BlockSpec((ta, kw), lambda b, h, s, qi, kj: (b * nq + qi[s], h))
    kv_spec = pl.BlockSpec((ta, kw), lambda b, h, s, qi, kj: (b * nq + kj[s], h))
    o_spec = pl.BlockSpec((ta, MLA_V), lambda b, h, s, qi, kj: (b * nq + qi[s], h))
    return pl.pallas_call(
        functools.partial(_flash_kernel, chain=chain),
        grid_spec=pltpu.PrefetchScalarGridSpec(
            num_scalar_prefetch=2, grid=(batch, MLA_N_HEADS, len(pairs)),
            in_specs=[q_spec, kv_spec, kv_spec],
            out_specs=o_spec,
            scratch_shapes=[pltpu.VMEM((ta, 1), F32), pltpu.VMEM((ta, kw), F32)]),
        out_shape=jax.ShapeDtypeStruct((t, MLA_N_HEADS * MLA_V), BF16),
        compiler_params=_params("arbitrary", "arbitrary", "arbitrary"),
        name="mla_flash_attention",
    )(qi, kj, q, k, v)


def _pad_lanes(w):
    k, nh, r = w.shape
    return jnp.zeros((k, nh, LANES), w.dtype).at[:, :, :r].set(w).reshape(k, nh * LANES)


def kernel(x, positions, ln_g, ln_b, ssm_w_in, ssm_conv_w, ssm_conv_b, ssm_dt_bias, ssm_a_log, ssm_d, ssm_norm_g, ssm_w_out, mla_w_kv_a, mla_kv_norm_g, mla_w_kv_b, mla_w_q_a, mla_q_norm_g, mla_w_q_b, mla_w_o, ffn_w_gate, ffn_w_up, ffn_w_down, moe_w_router, moe_b_router, moe_w_gate, moe_w_up, moe_w_down):
    batch, seq, d = x.shape
    t = batch * seq
    n_a = ssm_w_in.shape[0]
    n_heads = ssm_a_log.shape[1]
    d_inner = n_heads * SSM_HEAD_DIM
    conv_dim = ssm_conv_w.shape[2]
    ta = min(ATTN_TILE, seq)
    assert seq % SSM_CHUNK == 0 and t % TOKEN_TILE == 0 and t % min(MOE_TOKEN_TILE, t) == 0
    assert seq % ta == 0 and ta % min(ATTN_CHAIN, ta) == 0
    assert n_heads <= LANES and n_heads % (2 * SSM_N_GROUPS) == 0

    h = x.reshape(t, d)
    hb = h.astype(BF16)
    pos = positions.reshape(t, 1).astype(F32)
    inv_freq = ROPE_BASE ** (-jnp.arange(0, MLA_ROPE, 2, dtype=F32) / MLA_ROPE)
    invf = jnp.tile(inv_freq, LANES // (MLA_ROPE // 2)).reshape(1, LANES)
    vec = lambda a: a.reshape(1, -1)
    pad_heads = lambda a: jnp.zeros((1, LANES), F32).at[0, :n_heads].set(a)

    k_cat = v_cat = rope_cos = rope_sin = None
    for layer in range(DEPTH):
        g0, b0 = vec(ln_g[layer, 0]), vec(ln_b[layer, 0])
        g1, b1 = vec(ln_g[layer, 1]), vec(ln_b[layer, 1])
        if layer < n_a:
            a = layer
            w_in = ssm_w_in[a].astype(BF16)
            wz = w_in[:, :d_inner]
            wx = w_in[:, d_inner:d_inner + conv_dim]
            wdt = jnp.zeros((d, LANES), BF16).at[:, :n_heads].set(w_in[:, d_inner + conv_dim:])
            z, xbc, dt = _inproj(hb, wz, wx, wdt)
            y = _ssd(z, xbc, dt, ssm_conv_w[a], vec(ssm_conv_b[a]), pad_heads(ssm_dt_bias[a]),
                     pad_heads(ssm_a_log[a]), vec(jnp.repeat(ssm_d[a], SSM_HEAD_DIM)),
                     vec(ssm_norm_g[a]), batch=batch, seq=seq, n_heads=n_heads)
            h, hb = _proj_ln(y, ssm_w_out[a].astype(BF16), h, g0, b0)
        else:
            bi = layer - n_a
            cat_width = MLA_N_HEADS * 2 * LANES
            if bi == 0:
                kv_lora = mla_kv_norm_g.shape[0]
                wc = mla_w_kv_a[:, :kv_lora].astype(BF16)
                wr = _pad_lanes(mla_w_kv_a[:, kv_lora:].reshape(d, 1, MLA_ROPE)).astype(BF16)
                wkv = mla_w_kv_b.reshape(kv_lora, MLA_N_HEADS, MLA_NOPE + MLA_V)
                wkn = wkv[:, :, :MLA_NOPE].reshape(kv_lora, -1).astype(BF16)
                wv = wkv[:, :, MLA_NOPE:].reshape(kv_lora, -1).astype(BF16)
                k_cat, v_cat, rope_cos, rope_sin = _token_call(
                    _mla_kv_kernel, "mla_shared_kv", t,
                    [("row", hb), ("row", pos), ("full", invf), ("full", wc), ("full", wr),
                     ("full", vec(mla_kv_norm_g)), ("full", wkn), ("full", wv)],
                    [(cat_width, BF16), (cat_width, BF16), (LANES, F32), (LANES, F32)])
            q_lora = mla_q_norm_g.shape[1]
            wqb = mla_w_q_b[bi].reshape(q_lora, MLA_N_HEADS, MLA_QK)
            wqn = wqb[:, :, :MLA_NOPE].reshape(q_lora, -1).astype(BF16)
            wqr = _pad_lanes(wqb[:, :, MLA_NOPE:]).astype(BF16)
            (q_cat,) = _token_call(
                _mla_q_kernel, "mla_q_proj", t,
                [("row", hb), ("row", rope_cos), ("row", rope_sin),
                 ("full", mla_w_q_a[bi].astype(BF16)), ("full", vec(mla_q_norm_g[bi])),
                 ("full", wqn), ("full", wqr)],
                [(cat_width, BF16)])
            o = _flash_attention(q_cat, k_cat, v_cat, batch=batch, seq=seq)
            h, hb = _proj_ln(o, mla_w_o[bi].astype(BF16), h, g0, b0)
        ci = layer // 2
        if layer % 2 == 0:
            h, hb = _ffn_dense(hb, ffn_w_gate[ci].astype(BF16), ffn_w_up[ci].astype(BF16),
                               ffn_w_down[ci].astype(BF16), h, g1, b1)
        else:
            h, hb = _moe_layer(h, moe_w_router[ci], moe_b_router[ci],
                               moe_w_gate[ci].astype(BF16), moe_w_up[ci].astype(BF16),
                               moe_w_down[ci].astype(BF16), g1, b1)
    return h.reshape(batch, seq, d)
```

```python
import functools
import math

import numpy as np
import jax
import jax.numpy as jnp
from jax import lax
from jax.experimental import pallas as pl
from jax.experimental.pallas import tpu as pltpu

F32 = jnp.float32
BF16 = jnp.bfloat16

DEPTH = 4
DN_ALPHA = (2.0 * DEPTH) ** 0.25
LN_EPS = 1e-5
RMS_EPS = 1e-6
SSM_HEAD_DIM = 64
SSM_N_GROUPS = 4
SSM_D_STATE = 128
SSM_CHUNK = 128
SSM_CONV = 4
MLA_N_HEADS = 8
MLA_NOPE = 128
MLA_ROPE = 64
MLA_V = 128
MLA_QK = MLA_NOPE + MLA_ROPE
MLA_SCALE = MLA_QK ** -0.5
ROPE_BASE = 10000.0
N_EXPERTS = 8
TOP_K = 2
LOG2_E = math.log2(math.e)

LANES = 128
SUBLANES = 8
VMEM_LIMIT = 56 * 1024 * 1024

TOKEN_TILE = 512
FF_CHUNK = 256
MOE_TOKEN_TILE = 1024
ATTN_TILE = 2048
ATTN_CHAIN = 256


def _params(*sem):
    return pltpu.CompilerParams(dimension_semantics=sem, vmem_limit_bytes=VMEM_LIMIT)


def _dot(a, b):
    return jnp.dot(a, b, preferred_element_type=F32)


def _silu(v):
    return v * (1.0 / (1.0 + jnp.exp(-v)))


def _layer_norm(v, g, b):
    mu = jnp.mean(v, axis=-1, keepdims=True)
    vc = v - mu
    var = jnp.mean(vc * vc, axis=-1, keepdims=True)
    return vc * lax.rsqrt(var + LN_EPS) * g + b


def _lane_iota(shape):
    return lax.broadcasted_iota(jnp.int32, shape, len(shape) - 1)


def _inproj_kernel(x_ref, wz_ref, wx_ref, wdt_ref, z_ref, xbc_ref, dt_ref):
    x = x_ref[...]
    z_ref[...] = _dot(x, wz_ref[...]).astype(z_ref.dtype)
    xbc_ref[...] = _dot(x, wx_ref[...]).astype(xbc_ref.dtype)
    dt_ref[...] = _dot(x, wdt_ref[...])


def _inproj(xb, wz, wx, wdt):
    t, d = xb.shape
    tm = TOKEN_TILE
    full = lambda w: pl.BlockSpec(w.shape, lambda i: (0, 0))
    row = lambda n: pl.BlockSpec((tm, n), lambda i: (i, 0))
    return pl.pallas_call(
        _inproj_kernel,
        grid=(t // tm,),
        in_specs=[row(d), full(wz), full(wx), full(wdt)],
        out_specs=[row(wz.shape[1]), row(wx.shape[1]), row(wdt.shape[1])],
        out_shape=[jax.ShapeDtypeStruct((t, wz.shape[1]), BF16),
                   jax.ShapeDtypeStruct((t, wx.shape[1]), BF16),
                   jax.ShapeDtypeStruct((t, wdt.shape[1]), F32)],
        compiler_params=_params("arbitrary"),
        name="ssm_in_proj",
    )(xb, wz, wx, wdt)


def _ssd_kernel(z_ref, xbc_ref, dt_ref, cw_ref, cb_ref, dtb_ref, alog_ref, dsk_ref, ng_ref,
                y_ref, ubuf, xc, ybuf, state, *, n_heads, d_inner, gn):
    L = SSM_CHUNK
    P = SSM_HEAD_DIM
    n_state = SSM_D_STATE
    conv_dim = d_inner + 2 * gn
    halo = SUBLANES
    c = pl.program_id(1)

    @pl.when(c == 0)
    def _():
        ubuf[0:halo, :] = jnp.zeros((halo, conv_dim), F32)
        state[...] = jnp.zeros(state.shape, F32)

    ubuf[halo:halo + L, :] = xbc_ref[...].astype(F32)
    cblk = 512
    for j in range(conv_dim // cblk):
        cs = slice(j * cblk, (j + 1) * cblk)
        acc = cb_ref[:, cs] + cw_ref[3:4, cs] * ubuf[halo:halo + L, cs]
        for k in range(SSM_CONV - 1):
            sh = SSM_CONV - 1 - k
            acc = acc + cw_ref[k:k + 1, cs] * ubuf[halo - sh:halo - sh + L, cs]
        xc[:, cs] = _silu(acc)
    ubuf[0:halo, :] = ubuf[L:L + halo, :]

    lane = _lane_iota((L, LANES))
    head_ok = lane < n_heads
    dtr = dt_ref[...] + dtb_ref[...]
    dtp = jnp.maximum(dtr, 0.0) + jnp.log1p(jnp.exp(-jnp.abs(dtr)))
    dtp = jnp.where(head_ok, dtp, 0.0)
    a_row = jnp.where(head_ok[0:1, :], -jnp.exp(alog_ref[...]), 0.0)
    la = dtp * a_row
    row_i = lax.broadcasted_iota(jnp.int32, (L, L), 0)
    col_i = lax.broadcasted_iota(jnp.int32, (L, L), 1)
    causal = row_i >= col_i
    tril = jnp.where(causal, 1.0, 0.0).astype(F32)
    cum = jnp.dot(tril, la, preferred_element_type=F32, precision=lax.Precision.HIGHEST)
    cum_t = cum.T
    dt_t = dtp.T
    ecum = jnp.exp(cum)
    last_t = cum_t[:, L - 1:L]
    w_t = jnp.exp(last_t - cum_t) * dt_t
    edec_t = jnp.exp(last_t)
    lane_lo = _lane_iota((L, 2 * P)) < P

    b_off = d_inner
    c_off = d_inner + gn
    heads_per_group = n_heads // SSM_N_GROUPS
    for g in range(SSM_N_GROUPS):
        bg = xc[:, b_off + g * n_state: b_off + (g + 1) * n_state]
        cg = xc[:, c_off + g * n_state: c_off + (g + 1) * n_state]
        bg_t = bg.T
        cb = _dot(cg.astype(BF16), bg_t.astype(BF16))
        for pr in range(heads_per_group // 2):
            h0 = g * heads_per_group + 2 * pr
            pair = h0 // 2
            xs_pair = xc[:, h0 * P:(h0 + 2) * P].astype(BF16)
            st_pair = state[pair]
            rhs = jnp.concatenate([xs_pair, st_pair.astype(BF16)], axis=0)
            ys, sts, decs = [], [], []
            for h in (h0, h0 + 1):
                seg = cum[:, h:h + 1] - cum_t[h:h + 1, :]
                dec = jnp.where(causal, jnp.exp(seg), 0.0)
                gmat = cb * dec * dt_t[h:h + 1, :]
                cs_h = cg * ecum[:, h:h + 1]
                lhs = jnp.concatenate([gmat.astype(BF16), cs_h.astype(BF16)], axis=1)
                ys.append(_dot(lhs, rhs))
                bw_t = (bg_t * w_t[h:h + 1, :]).astype(BF16)
                sts.append(_dot(bw_t, xs_pair))
                decs.append(edec_t[h:h + 1, :])
            ybuf[:, h0 * P:(h0 + 2) * P] = jnp.where(lane_lo, ys[0], ys[1])
            dec_pair = jnp.where(lane_lo, decs[0], decs[1])
            state[pair] = dec_pair * st_pair + jnp.where(lane_lo, sts[0], sts[1])

    gw = d_inner // SSM_N_GROUPS
    for g in range(SSM_N_GROUPS):
        cs = slice(g * gw, (g + 1) * gw)
        y = ybuf[:, cs] + dsk_ref[:, cs] * xc[:, cs]
        y = y * _silu(z_ref[:, cs].astype(F32))
        ms = jnp.mean(y * y, axis=-1, keepdims=True)
        y_ref[:, cs] = (y * lax.rsqrt(ms + LN_EPS) * ng_ref[:, cs]).astype(y_ref.dtype)


def _ssd(z, xbc, dt, conv_w, conv_b, dt_bias, a_log, d_skip, norm_g, *, batch, seq, n_heads):
    t, d_inner = z.shape
    conv_dim = xbc.shape[1]
    gn = (conv_dim - d_inner) // 2
    L = SSM_CHUNK
    nc = seq // L
    row = lambda n: pl.BlockSpec((L, n), lambda b, c: (b * nc + c, 0))
    full = lambda a: pl.BlockSpec(a.shape, lambda b, c: (0, 0))
    kern = functools.partial(_ssd_kernel, n_heads=n_heads, d_inner=d_inner, gn=gn)
    return pl.pallas_call(
        kern,
        grid=(batch, nc),
        in_specs=[row(d_inner), row(conv_dim), row(LANES), full(conv_w), full(conv_b),
                  full(dt_bias), full(a_log), full(d_skip), full(norm_g)],
        out_specs=row(d_inner),
        out_shape=jax.ShapeDtypeStruct((t, d_inner), BF16),
        scratch_shapes=[pltpu.VMEM((L + 2 * SUBLANES, conv_dim), F32),
                        pltpu.VMEM((L, conv_dim), F32),
                        pltpu.VMEM((L, d_inner), F32),
                        pltpu.VMEM((n_heads // 2, SSM_D_STATE, 2 * SSM_HEAD_DIM), F32)],
        compiler_params=_params("arbitrary", "arbitrary"),
        name="ssm_conv_ssd",
    )(z, xbc, dt, conv_w, conv_b, dt_bias, a_log, d_skip, norm_g)


def _proj_ln_kernel(x_ref, w_ref, res_ref, g_ref, b_ref, of_ref, ob_ref):
    mix = _dot(x_ref[...], w_ref[...])
    out = _layer_norm(DN_ALPHA * res_ref[...] + mix, g_ref[...], b_ref[...])
    of_ref[...] = out
    ob_ref[...] = out.astype(BF16)


def _proj_ln(xb, w, res, g, b):
    t, k = xb.shape
    d = w.shape[1]
    tm = TOKEN_TILE
    row = lambda n: pl.BlockSpec((tm, n), lambda i: (i, 0))
    full = lambda a: pl.BlockSpec(a.shape, lambda i: (0, 0))
    return pl.pallas_call(
        _proj_ln_kernel,
        grid=(t // tm,),
        in_specs=[row(k), full(w), row(d), full(g), full(b)],
        out_specs=[row(d), row(d)],
        out_shape=[jax.ShapeDtypeStruct((t, d), F32), jax.ShapeDtypeStruct((t, d), BF16)],
        compiler_params=_params("arbitrary"),
        name="proj_residual_ln",
    )(xb, w, res, g, b)


def _ffn_tile(x, wg_ref, wu_ref, wd_ref):
    dff = wg_ref.shape[2]
    acc = None
    for c in range(dff // FF_CHUNK):
        cs = slice(c * FF_CHUNK, (c + 1) * FF_CHUNK)
        gate = _dot(x, wg_ref[0, :, cs])
        up = _dot(x, wu_ref[0, :, cs])
        hid = (_silu(gate) * up).astype(BF16)
        part = _dot(hid, wd_ref[0, cs, :])
        acc = part if acc is None else acc + part
    return acc


def _ffn_dense_kernel(te_ref, nt_ref, x_ref, wg_ref, wu_ref, wd_ref, res_ref, g_ref, b_ref,
                      of_ref, ob_ref):
    ffn = _ffn_tile(x_ref[...], wg_ref, wu_ref, wd_ref)
    out = _layer_norm(DN_ALPHA * res_ref[...] + ffn, g_ref[...], b_ref[...])
    of_ref[...] = out
    ob_ref[...] = out.astype(BF16)


def _ffn_moe_kernel(te_ref, nt_ref, x_ref, wg_ref, wu_ref, wd_ref, o_ref):
    used = pl.program_id(0) < nt_ref[0]

    @pl.when(used)
    def _():
        o_ref[...] = _ffn_tile(x_ref[...].astype(BF16), wg_ref, wu_ref, wd_ref)

    @pl.when(jnp.logical_not(used))
    def _():
        o_ref[...] = jnp.zeros(o_ref.shape, F32)


def _ffn_specs(d, dff, tm):
    assert dff % FF_CHUNK == 0
    x_spec = pl.BlockSpec((tm, d), lambda i, te, nt: (i, 0))
    wgu_spec = pl.BlockSpec((1, d, dff), lambda i, te, nt: (te[i], 0, 0))
    wd_spec = pl.BlockSpec((1, dff, d), lambda i, te, nt: (te[i], 0, 0))
    return x_spec, wgu_spec, wd_spec


def _ffn_dense(xb, wg, wu, wd, res, g, b):
    t, d = xb.shape
    tm = TOKEN_TILE
    nt = t // tm
    x_spec, wgu_spec, wd_spec = _ffn_specs(d, wg.shape[-1], tm)
    vec = pl.BlockSpec((1, d), lambda i, te, n: (0, 0))
    te = jnp.zeros((nt,), jnp.int32)
    n_used = jnp.full((1,), nt, jnp.int32)
    return pl.pallas_call(
        _ffn_dense_kernel,
        grid_spec=pltpu.PrefetchScalarGridSpec(
            num_scalar_prefetch=2, grid=(nt,),
            in_specs=[x_spec, wgu_spec, wgu_spec, wd_spec, x_spec, vec, vec],
            out_specs=[x_spec, x_spec]),
        out_shape=[jax.ShapeDtypeStruct((t, d), F32), jax.ShapeDtypeStruct((t, d), BF16)],
        compiler_params=_params("arbitrary"),
        name="ffn_dense_ln",
    )(te, n_used, xb, wg[None], wu[None], wd[None], res, g, b)


def _ffn_moe(xs, wg, wu, wd, tile_expert, n_used):
    p, d = xs.shape
    tm = TOKEN_TILE
    nt = p // tm
    x_spec, wgu_spec, wd_spec = _ffn_specs(d, wg.shape[-1], tm)
    return pl.pallas_call(
        _ffn_moe_kernel,
        grid_spec=pltpu.PrefetchScalarGridSpec(
            num_scalar_prefetch=2, grid=(nt,),
            in_specs=[x_spec, wgu_spec, wgu_spec, wd_spec],
            out_specs=x_spec),
        out_shape=jax.ShapeDtypeStruct((p, d), F32),
        compiler_params=_params("arbitrary"),
        name="ffn_moe_grouped",
    )(tile_expert, n_used, xs, wg, wu, wd)


def _router_kernel(h_ref, w_ref, b_ref, idx_ref, wt_ref, *, n_experts):
    logits = jnp.dot(h_ref[...], w_ref[...], preferred_element_type=F32,
                     precision=lax.Precision.HIGHEST) + b_ref[...]
    lane = _lane_iota(logits.shape)
    neg = jnp.float32(-jnp.inf)
    logits = jnp.where(lane < n_experts, logits, neg)
    m1 = jnp.max(logits, axis=-1, keepdims=True)
    i1 = jnp.min(jnp.where(logits == m1, lane, LANES), axis=-1, keepdims=True)
    rest = jnp.where(lane == i1, neg, logits)
    m2 = jnp.max(rest, axis=-1, keepdims=True)
    i2 = jnp.min(jnp.where(rest == m2, lane, LANES), axis=-1, keepdims=True)
    e2 = jnp.exp(m2 - m1)
    w1 = 1.0 / (1.0 + e2)
    w2 = e2 / (1.0 + e2)
    idx_ref[...] = jnp.where(lane == 0, i1, jnp.where(lane == 1, i2, 0))
    wt_ref[...] = jnp.where(lane == 0, w1, jnp.where(lane == 1, w2, 0.0))


def _router(h, w_pad, b_pad, n_experts):
    t, d = h.shape
    tm = TOKEN_TILE
    row = lambda n: pl.BlockSpec((tm, n), lambda i: (i, 0))
    full = lambda a: pl.BlockSpec(a.shape, lambda i: (0, 0))
    return pl.pallas_call(
        functools.partial(_router_kernel, n_experts=n_experts),
        grid=(t // tm,),
        in_specs=[row(d), full(w_pad), full(b_pad)],
        out_specs=[row(LANES), row(LANES)],
        out_shape=[jax.ShapeDtypeStruct((t, LANES), jnp.int32),
                   jax.ShapeDtypeStruct((t, LANES), F32)],
        compiler_params=_params("arbitrary"),
        name="moe_router",
    )(h, w_pad, b_pad)


def _stage_slots(slots_hbm, tile, slots_smem, sem):
    stage = pltpu.make_async_copy(slots_hbm.at[tile], slots_smem, sem)
    stage.start()
    stage.wait()


def _dispatch_copy(h_ref, xs_hbm, r, dst_row, sem):
    return pltpu.make_async_copy(h_ref.at[pl.ds(r, 1), :], xs_hbm.at[pl.ds(dst_row, 1), :], sem)


def _moe_dispatch_kernel(slots_hbm, h_ref, xs_in, xs_hbm, slots_smem, slot_sem, row_sem):
    del xs_in
    tm = h_ref.shape[0]
    _stage_slots(slots_hbm, pl.program_id(0), slots_smem, slot_sem)

    def issue(r, carry):
        _dispatch_copy(h_ref, xs_hbm, r, slots_smem[r], row_sem).start(priority=0)
        _dispatch_copy(h_ref, xs_hbm, r, slots_smem[tm + r], row_sem).start(priority=1)
        return carry

    lax.fori_loop(0, tm, issue, 0, unroll=8)

    def drain(r, carry):
        _dispatch_copy(h_ref, xs_hbm, r, 0, row_sem).wait()
        _dispatch_copy(h_ref, xs_hbm, r, 0, row_sem).wait()
        return carry

    lax.fori_loop(0, tm, drain, 0, unroll=8)


def _moe_dispatch(h, slots, n_rows):
    t, d = h.shape
    nt, two_tm = slots.shape
    tm = two_tm // TOP_K
    anyspec = pl.BlockSpec(memory_space=pl.ANY)
    return pl.pallas_call(
        _moe_dispatch_kernel,
        grid=(nt,),
        in_specs=[anyspec, pl.BlockSpec((tm, d), lambda i: (i, 0)), anyspec],
        out_specs=anyspec,
        out_shape=jax.ShapeDtypeStruct((n_rows, d), F32),
        scratch_shapes=[pltpu.SMEM((two_tm,), jnp.int32), pltpu.SemaphoreType.DMA,
                        pltpu.SemaphoreType.DMA],
        input_output_aliases={2: 0},
        compiler_params=_params("arbitrary"),
        name="moe_dispatch",
    )(slots, h, jnp.zeros((n_rows, d), F32))


def _combine_copy(ys_hbm, src_row, buf, slot, r, sems):
    return pltpu.make_async_copy(ys_hbm.at[pl.ds(src_row, 1), :],
                                 buf.at[slot, pl.ds(r, 1), :], sems.at[slot])


def _moe_combine_kernel(slots_hbm, ys_hbm, wt_ref, res_ref, g_ref, b_ref, of_ref, ob_ref,
                        buf, slots_smem, slot_sem, sems):
    i = pl.program_id(0)
    n = pl.num_programs(0)
    tm = res_ref.shape[0]
    n_rows = TOP_K * tm

    def issue_tile(tile, slot):
        _stage_slots(slots_hbm, tile, slots_smem, slot_sem)

        def issue(j, carry):
            r = 2 * j
            _combine_copy(ys_hbm, slots_smem[r], buf, slot, r, sems).start(priority=0)
            _combine_copy(ys_hbm, slots_smem[r + 1], buf, slot, r + 1, sems).start(priority=1)
            return carry

        lax.fori_loop(0, n_rows // 2, issue, 0, unroll=4)

    @pl.when(i == 0)
    def _():
        issue_tile(0, 0)

    @pl.when(i + 1 < n)
    def _():
        issue_tile(i + 1, (i + 1) % 2)

    slot = i % 2

    def drain(r, carry):
        _combine_copy(ys_hbm, 0, buf, slot, r, sems).wait()
        return carry

    lax.fori_loop(0, n_rows, drain, 0, unroll=8)

    w0 = wt_ref[:, 0:1]
    w1 = wt_ref[:, 1:2]
    ffn = w0 * buf[slot, 0:tm, :] + w1 * buf[slot, tm:n_rows, :]
    out = _layer_norm(DN_ALPHA * res_ref[...] + ffn, g_ref[...], b_ref[...])
    of_ref[...] = out
    ob_ref[...] = out.astype(BF16)


def _moe_combine(ys, slots, wts, res, g, b):
    t, d = res.shape
    nt, two_tm = slots.shape
    tm = two_tm // TOP_K
    row = lambda n: pl.BlockSpec((tm, n), lambda i: (i, 0))
    vec = pl.BlockSpec((1, d), lambda i: (0, 0))
    anyspec = pl.BlockSpec(memory_space=pl.ANY)
    return pl.pallas_call(
        _moe_combine_kernel,
        grid=(nt,),
        in_specs=[anyspec, anyspec, row(LANES), row(d), vec, vec],
        out_specs=[row(d), row(d)],
        out_shape=[jax.ShapeDtypeStruct((t, d), F32), jax.ShapeDtypeStruct((t, d), BF16)],
        scratch_shapes=[pltpu.VMEM((2, two_tm, d), F32), pltpu.SMEM((two_tm,), jnp.int32),
                        pltpu.SemaphoreType.DMA, pltpu.SemaphoreType.DMA((2,))],
        compiler_params=_params("arbitrary"),
        name="moe_combine_ln",
    )(slots, ys, wts, res, g, b)


def _moe_slot_tables(idx, tm, tok):
    t = idx.shape[0]
    n_slots = TOP_K * t
    n_rows = n_slots + N_EXPERTS * tm
    nt = n_rows // tm
    ef = idx.reshape(-1)
    onehot = (ef[:, None] == jnp.arange(N_EXPERTS, dtype=jnp.int32)[None, :]).astype(jnp.int32)
    csum = jnp.cumsum(onehot, axis=0)
    rank = jnp.sum(csum * onehot, axis=1) - 1
    counts = csum[-1]
    padded = ((counts + tm - 1) // tm) * tm
    ends = jnp.cumsum(padded)
    starts = ends - padded
    dest = jnp.sum(starts[None, :] * onehot, axis=1) + rank
    n_used = (ends[-1] // tm).astype(jnp.int32).reshape(1)
    tile_start = jnp.arange(nt, dtype=jnp.int32) * tm
    tile_expert = jnp.sum((tile_start[:, None] >= ends[None, :]).astype(jnp.int32), axis=1)
    tile_expert = jnp.minimum(tile_expert, N_EXPERTS - 1).astype(jnp.int32)
    slots = dest.reshape(t // tok, tok, TOP_K).transpose(0, 2, 1).reshape(t // tok, TOP_K * tok)
    return slots.astype(jnp.int32), tile_expert, n_used, n_rows


def _moe_layer(h, w_router, b_router, wg, wu, wd, g, b):
    t, d = h.shape
    tm = TOKEN_TILE
    w_pad = jnp.zeros((d, LANES), F32).at[:, :N_EXPERTS].set(w_router)
    b_pad = jnp.zeros((1, LANES), F32).at[0, :N_EXPERTS].set(b_router)
    idx, wts = _router(h, w_pad, b_pad, N_EXPERTS)
    slots, tile_expert, n_used, n_rows = _moe_slot_tables(
        idx[:, :TOP_K], tm, min(MOE_TOKEN_TILE, t))
    xs = _moe_dispatch(h, slots, n_rows)
    ys = _ffn_moe(xs, wg, wu, wd, tile_expert, n_used)
    return _moe_combine(ys, slots, wts, h, g, b)


def _rope_tables(pos_ref, invf_ref):
    ang = pos_ref[...] * invf_ref[...]
    return jnp.cos(ang), jnp.sin(ang)


def _rope_block(v, cos, sin):
    half = MLA_ROPE // 2
    lane = _lane_iota(v.shape)
    up = pltpu.roll(v, LANES - half, 1)
    down = pltpu.roll(v, half, 1)
    rot = jnp.where(lane < half, -up, down)
    return jnp.where(lane < MLA_ROPE, v * cos + rot * sin, 0.0)


def _rms(v, g):
    return v * lax.rsqrt(jnp.mean(v * v, axis=-1, keepdims=True) + RMS_EPS) * g


def _mla_kv_kernel(h_ref, pos_ref, invf_ref, wc_ref, wr_ref, g_ref, wkn_ref, wv_ref,
                   k_ref, v_ref, cos_ref, sin_ref):
    hb = h_ref[...]
    ckv = _rms(_dot(hb, wc_ref[...]), g_ref[...]).astype(BF16)
    cos, sin = _rope_tables(pos_ref, invf_ref)
    cos_ref[...] = cos
    sin_ref[...] = sin
    kr = _rope_block(_dot(hb, wr_ref[...]), cos, sin).astype(BF16)
    kn = _dot(ckv, wkn_ref[...]).astype(BF16)
    vv = _dot(ckv, wv_ref[...]).astype(BF16)
    ones = jnp.ones((hb.shape[0], LANES), BF16)
    for hd in range(MLA_N_HEADS):
        lo, mid, hi = hd * 2 * LANES, hd * 2 * LANES + LANES, (hd + 1) * 2 * LANES
        k_ref[:, lo:mid] = kn[:, hd * LANES:(hd + 1) * LANES]
        k_ref[:, mid:hi] = kr
        v_ref[:, lo:mid] = vv[:, hd * LANES:(hd + 1) * LANES]
        v_ref[:, mid:hi] = ones


def _mla_q_kernel(h_ref, cos_ref, sin_ref, wa_ref, g_ref, wqn_ref, wqr_ref, q_ref):
    cq = _rms(_dot(h_ref[...], wa_ref[...]), g_ref[...]).astype(BF16)
    cos, sin = cos_ref[...], sin_ref[...]
    scale = MLA_SCALE * LOG2_E
    qn = _dot(cq, wqn_ref[...]) * scale
    qr = _dot(cq, wqr_ref[...]) * scale
    for hd in range(MLA_N_HEADS):
        lo, mid, hi = hd * 2 * LANES, hd * 2 * LANES + LANES, (hd + 1) * 2 * LANES
        q_ref[:, lo:mid] = qn[:, hd * LANES:(hd + 1) * LANES].astype(BF16)
        q_ref[:, mid:hi] = _rope_block(qr[:, hd * LANES:(hd + 1) * LANES], cos, sin).astype(BF16)


def _token_call(kernel, name, t, operands, outs):
    tm = TOKEN_TILE
    row = lambda n: pl.BlockSpec((tm, n), lambda i: (i, 0))
    full = lambda a: pl.BlockSpec(a.shape, lambda i: (0, 0))
    return pl.pallas_call(
        kernel,
        grid=(t // tm,),
        in_specs=[row(a.shape[1]) if kind == "row" else full(a) for kind, a in operands],
        out_specs=[row(n) for n, _ in outs],
        out_shape=[jax.ShapeDtypeStruct((t, n), dt) for n, dt in outs],
        compiler_params=_params("arbitrary"),
        name=name,
    )(*[a for _, a in operands])


def _flash_kernel(qi_ref, kj_ref, q_ref, k_ref, v_ref, o_ref, m_ref, acc_ref, *, chain):
    s_idx = pl.program_id(2)
    qi = qi_ref[s_idx]
    kj = kj_ref[s_idx]
    tq = q_ref.shape[0]
    n_chain = tq // chain

    @pl.when(kj == 0)
    def _():
        m_ref[...] = jnp.full(m_ref.shape, -jnp.inf, F32)
        acc_ref[...] = jnp.zeros(acc_ref.shape, F32)

    def scores(rows, lo, hi):
        return lax.dot_general(q_ref[rows, :], k_ref[lo:hi, :], (((1,), (1,)), ((), ())),
                               preferred_element_type=F32)

    def chain_update(r, n_open, masked):
        rows = slice(r * chain, (r + 1) * chain)
        m_new = m_old = m_ref[rows, :]
        parts = []
        if n_open:
            parts.append((scores(rows, 0, n_open), 0, n_open))
        if masked:
            ri = lax.broadcasted_iota(jnp.int32, (chain, chain), 0)
            ci = lax.broadcasted_iota(jnp.int32, (chain, chain), 1)
            sc = jnp.where(ci <= ri, scores(rows, n_open, n_open + chain), -jnp.inf)
            parts.append((sc, n_open, n_open + chain))
        for sc, _, _ in parts:
            m_new = jnp.maximum(m_new, jnp.max(sc, axis=-1, keepdims=True))
        acc = jnp.exp2(m_old - m_new) * acc_ref[rows, :]
        for sc, lo, hi in parts:
            acc = acc + _dot(jnp.exp2(sc - m_new).astype(BF16), v_ref[lo:hi, :])
        acc_ref[rows, :] = acc
        m_ref[rows, :] = m_new

    @pl.when(kj < qi)
    def _():
        for r in range(n_chain):
            chain_update(r, k_ref.shape[0], False)

    @pl.when(kj == qi)
    def _():
        for r in reversed(range(n_chain)):
            chain_update(r, r * chain, True)
        o_ref[...] = (acc_ref[:, 0:MLA_V] / acc_ref[:, MLA_V:2 * MLA_V]).astype(o_ref.dtype)


def _flash_attention(q, k, v, *, batch, seq):
    t = q.shape[0]
    ta = min(ATTN_TILE, seq)
    chain = min(ATTN_CHAIN, ta)
    nq = seq // ta
    pairs = [(i, j) for i in range(nq) for j in range(i + 1)]
    qi = jnp.asarray(np.array([p[0] for p in pairs], np.int32))
    kj = jnp.asarray(np.array([p[1] for p in pairs], np.int32))
    kw = 2 * LANES
    q_spec = pl.BlockSpec((ta, kw), lambda b, h, s, qi, kj: (b * nq + qi[s], h))
    kv_spec = pl.BlockSpec((ta, kw), lambda b, h, s, qi, kj: (b * nq + kj[s], h))
    o_spec = pl.BlockSpec((ta, MLA_V), lambda b, h, s, qi, kj: (b * nq + qi[s], h))
    return pl.pallas_call(
        functools.partial(_flash_kernel, chain=chain),
        grid_spec=pltpu.PrefetchScalarGridSpec(
            num_scalar_prefetch=2, grid=(batch, MLA_N_HEADS, len(pairs)),
            in_specs=[q_spec, kv_spec, kv_spec],
            out_specs=o_spec,
            scratch_shapes=[pltpu.VMEM((ta, 1), F32), pltpu.VMEM((ta, kw), F32)]),
        out_shape=jax.ShapeDtypeStruct((t, MLA_N_HEADS * MLA_V), BF16),
        compiler_params=_params("arbitrary", "arbitrary", "arbitrary"),
        name="mla_flash_attention",
    )(qi, kj, q, k, v)


def _pad_lanes(w):
    k, nh, r = w.shape
    return jnp.zeros((k, nh, LANES), w.dtype).at[:, :, :r].set(w).reshape(k, nh * LANES)


def kernel(x, positions, ln_g, ln_b, ssm_w_in, ssm_conv_w, ssm_conv_b, ssm_dt_bias, ssm_a_log, ssm_d, ssm_norm_g, ssm_w_out, mla_w_kv_a, mla_kv_norm_g, mla_w_kv_b, mla_w_q_a, mla_q_norm_g, mla_w_q_b, mla_w_o, ffn_w_gate, ffn_w_up, ffn_w_down, moe_w_router, moe_b_router, moe_w_gate, moe_w_up, moe_w_down):
    batch, seq, d = x.shape
    t = batch * seq
    n_a = ssm_w_in.shape[0]
    n_heads = ssm_a_log.shape[1]
    d_inner = n_heads * SSM_HEAD_DIM
    conv_dim = ssm_conv_w.shape[2]
    ta = min(ATTN_TILE, seq)
    assert seq % SSM_CHUNK == 0 and t % TOKEN_TILE == 0 and t % min(MOE_TOKEN_TILE, t) == 0
    assert seq % ta == 0 and ta % min(ATTN_CHAIN, ta) == 0
    assert n_heads <= LANES and n_heads % (2 * SSM_N_GROUPS) == 0

    h = x.reshape(t, d)
    hb = h.astype(BF16)
    pos = positions.reshape(t, 1).astype(F32)
    inv_freq = ROPE_BASE ** (-jnp.arange(0, MLA_ROPE, 2, dtype=F32) / MLA_ROPE)
    invf = jnp.tile(inv_freq, LANES // (MLA_ROPE // 2)).reshape(1, LANES)
    vec = lambda a: a.reshape(1, -1)
    pad_heads = lambda a: jnp.zeros((1, LANES), F32).at[0, :n_heads].set(a)

    k_cat = v_cat = rope_cos = rope_sin = None
    for layer in range(DEPTH):
        g0, b0 = vec(ln_g[layer, 0]), vec(ln_b[layer, 0])
        g1, b1 = vec(ln_g[layer, 1]), vec(ln_b[layer, 1])
        if layer < n_a:
            a = layer
            w_in = ssm_w_in[a].astype(BF16)
            wz = w_in[:, :d_inner]
            wx = w_in[:, d_inner:d_inner + conv_dim]
            wdt = jnp.zeros((d, LANES), BF16).at[:, :n_heads].set(w_in[:, d_inner + conv_dim:])
            z, xbc, dt = _inproj(hb, wz, wx, wdt)
            y = _ssd(z, xbc, dt, ssm_conv_w[a], vec(ssm_conv_b[a]), pad_heads(ssm_dt_bias[a]),
                     pad_heads(ssm_a_log[a]), vec(jnp.repeat(ssm_d[a], SSM_HEAD_DIM)),
                     vec(ssm_norm_g[a]), batch=batch, seq=seq, n_heads=n_heads)
            h, hb = _proj_ln(y, ssm_w_out[a].astype(BF16), h, g0, b0)
        else:
            bi = layer - n_a
            cat_width = MLA_N_HEADS * 2 * LANES
            if bi == 0:
                kv_lora = mla_kv_norm_g.shape[0]
                wc = mla_w_kv_a[:, :kv_lora].astype(BF16)
                wr = _pad_lanes(mla_w_kv_a[:, kv_lora:].reshape(d, 1, MLA_ROPE)).astype(BF16)
                wkv = mla_w_kv_b.reshape(kv_lora, MLA_N_HEADS, MLA_NOPE + MLA_V)
                wkn = wkv[:, :, :MLA_NOPE].reshape(kv_lora, -1).astype(BF16)
                wv = wkv[:, :, MLA_NOPE:].reshape(kv_lora, -1).astype(BF16)
                k_cat, v_cat, rope_cos, rope_sin = _token_call(
                    _mla_kv_kernel, "mla_shared_kv", t,
                    [("row", hb), ("row", pos), ("full", invf), ("full", wc), ("full", wr),
                     ("full", vec(mla_kv_norm_g)), ("full", wkn), ("full", wv)],
                    [(cat_width, BF16), (cat_width, BF16), (LANES, F32), (LANES, F32)])
            q_lora = mla_q_norm_g.shape[1]
            wqb = mla_w_q_b[bi].reshape(q_lora, MLA_N_HEADS, MLA_QK)
            wqn = wqb[:, :, :MLA_NOPE].reshape(q_lora, -1).astype(BF16)
            wqr = _pad_lanes(wqb[:, :, MLA_NOPE:]).astype(BF16)
            (q_cat,) = _token_call(
                _mla_q_kernel, "mla_q_proj", t,
                [("row", hb), ("row", rope_cos), ("row", rope_sin),
                 ("full", mla_w_q_a[bi].astype(BF16)), ("full", vec(mla_q_norm_g[bi])),
                 ("full", wqn), ("full", wqr)],
                [(cat_width, BF16)])
            o = _flash_attention(q_cat, k_cat, v_cat, batch=batch, seq=seq)
            h, hb = _proj_ln(o, mla_w_o[bi].astype(BF16), h, g0, b0)
        ci = layer // 2
        if layer % 2 == 0:
            h, hb = _ffn_dense(hb, ffn_w_gate[ci].astype(BF16), ffn_w_up[ci].astype(BF16),
                               ffn_w_down[ci].astype(BF16), h, g1, b1)
        else:
            h, hb = _moe_layer(h, moe_w_router[ci], moe_b_router[ci],
                               moe_w_gate[ci].astype(BF16), moe_w_up[ci].astype(BF16),
                               moe_w_down[ci].astype(BF16), g1, b1)
    return h.reshape(batch, seq, d)
```
